```python
import math
import jax, jax.numpy as jnp
from jax import lax
import numpy as np

D_MODEL = 4096
BATCH = 2
SEQ = 8192
DEPTH = 1

N_ATTN_HEADS = 16
ATTN_HEAD_DIM = 128
ATTN_WIDTH = N_ATTN_HEADS * ATTN_HEAD_DIM
Q_BLOCK = 128
SSM_GROUP = 16
SSM_STATE = 64
SSM_WIDTH = D_MODEL // 2
N_SSM_GROUPS = SSM_WIDTH // SSM_GROUP
SSM_CHUNK = 128
DT_MIN = 1e-3
DT_MAX = 1e-1
N_EXPERTS = 64
TOP_K = 8
D_EXPERT = 768
D_SHARED = 768
ROUTED_SCALE = 2.5
MOE_BLOCK = 256
NORM_EPS = 1e-6
N_MOD = 6

IN_WIDTH = 3 * ATTN_WIDTH + N_ATTN_HEADS + SSM_WIDTH + 2 * D_MODEL
IN_SPLITS = [ATTN_WIDTH, 2 * ATTN_WIDTH, 3 * ATTN_WIDTH,
             3 * ATTN_WIDTH + N_ATTN_HEADS,
             3 * ATTN_WIDTH + N_ATTN_HEADS + SSM_WIDTH,
             3 * ATTN_WIDTH + N_ATTN_HEADS + SSM_WIDTH + D_MODEL]

kernel_name = 'hybrid_fox_s5_moe_adaln_block'


def rms_norm(x, g):
    xf = x.astype(jnp.float32)
    y = xf * lax.rsqrt(jnp.mean(xf * xf, axis=-1, keepdims=True) + NORM_EPS)
    return (y * g.astype(jnp.float32)).astype(x.dtype)


def modulate(h, shift, scale):
    return h * (1 + scale[:, None, :]) + shift[:, None, :]


def forgetting_attention(q, k, v, log_f):
    b, s, h, dh = q.shape
    q = q.transpose(0, 2, 1, 3)
    k = k.transpose(0, 2, 1, 3)
    v = v.transpose(0, 2, 1, 3)
    cum = jnp.cumsum(log_f, axis=1).transpose(0, 2, 1)
    scale = dh ** -0.5
    k_pos = jnp.arange(s)
    n_blocks = s // Q_BLOCK

    def one_block(i):
        start = i * Q_BLOCK
        qb = lax.dynamic_slice_in_dim(q, start, Q_BLOCK, axis=2)
        cq = lax.dynamic_slice_in_dim(cum, start, Q_BLOCK, axis=2)
        logits = jnp.einsum('bhqd,bhkd->bhqk', qb, k,
                            preferred_element_type=jnp.float32) * scale
        logits = logits + cq[..., :, None] - cum[..., None, :]
        q_pos = start + jnp.arange(Q_BLOCK)
        causal = k_pos[None, :] <= q_pos[:, None]
        logits = jnp.where(causal, logits, -jnp.inf)
        p = jax.nn.softmax(logits, axis=-1)
        return jnp.einsum('bhqk,bhkd->bhqd', p.astype(v.dtype), v)

    out = lax.map(one_block, jnp.arange(n_blocks))
    return out.transpose(1, 0, 3, 2, 4).reshape(b, s, h * dh)


def _complex_affine_combine(e1, e2):
    a1r, a1i, b1r, b1i = e1
    a2r, a2i, b2r, b2i = e2
    ar = a2r * a1r - a2i * a1i
    ai = a2r * a1i + a2i * a1r
    br = a2r * b1r - a2i * b1i + b2r
    bi = a2r * b1i + a2i * b1r + b2i
    return ar, ai, br, bi


def s5_ssm(u, lam_re, lam_im, log_dt, b_re, b_im, c_re, c_im, d_skip):
    bsz, s, g, hg = u.shape
    p = lam_re.shape[-1]
    uf = u.astype(jnp.float32)
    lr = lam_re.astype(jnp.float32)
    li = lam_im.astype(jnp.float32)
    dt = jnp.exp(log_dt.astype(jnp.float32))[:, None]
    mag = jnp.exp(lr * dt)
    a_re = mag * jnp.cos(li * dt)
    a_im = mag * jnp.sin(li * dt)
    den = lr * lr + li * li
    k_re = ((a_re - 1) * lr + a_im * li) / den
    k_im = (a_im * lr - (a_re - 1) * li) / den
    br = b_re.astype(jnp.float32)
    bi = b_im.astype(jnp.float32)
    bb_re = k_re[..., None] * br - k_im[..., None] * bi
    bb_im = k_re[..., None] * bi + k_im[..., None] * br
    cr = c_re.astype(jnp.float32)
    ci = c_im.astype(jnp.float32)
    dd = d_skip.astype(jnp.float32)
    n_chunks = s // SSM_CHUNK
    u_chunks = uf.reshape(bsz, n_chunks, SSM_CHUNK, g, hg).transpose(1, 0, 2, 3, 4)
    a_re_b = jnp.broadcast_to(a_re, (bsz, SSM_CHUNK, g, p))
    a_im_b = jnp.broadcast_to(a_im, (bsz, SSM_CHUNK, g, p))

    def chunk_step(carry, u_c):
        h_re0, h_im0 = carry
        bu_re = jnp.einsum('blgh,gph->blgp', u_c, bb_re)
        bu_im = jnp.einsum('blgh,gph->blgp', u_c, bb_im)
        pw_re, pw_im, x_re, x_im = lax.associative_scan(
            _complex_affine_combine, (a_re_b, a_im_b, bu_re, bu_im), axis=1)
        h_re = x_re + pw_re * h_re0[:, None] - pw_im * h_im0[:, None]
        h_im = x_im + pw_re * h_im0[:, None] + pw_im * h_re0[:, None]
        y = (jnp.einsum('ghp,blgp->blgh', cr, h_re)
             - jnp.einsum('ghp,blgp->blgh', ci, h_im)) + dd * u_c
        return (h_re[:, -1], h_im[:, -1]), y

    init = (jnp.zeros((bsz, g, p), jnp.float32), jnp.zeros((bsz, g, p), jnp.float32))
    _, ys = lax.scan(chunk_step, init, u_chunks)
    return ys.transpose(1, 0, 2, 3, 4).reshape(bsz, s, g * hg)


def swiglu(h, w_gate, w_up, w_down):
    return (jax.nn.silu(h @ w_gate) * (h @ w_up)) @ w_down


def routed_experts(h, expert_idx, expert_w, w_gate, w_up, w_down):
    n, d = h.shape
    n_assign = n * TOP_K
    flat_e = expert_idx.reshape(-1)
    flat_tok = jnp.repeat(jnp.arange(n, dtype=jnp.int32), TOP_K)
    flat_w = expert_w.reshape(-1)
    order = jnp.argsort(flat_e)
    sorted_e = flat_e[order]
    counts = jnp.bincount(flat_e, length=N_EXPERTS)
    padded = (counts + MOE_BLOCK - 1) // MOE_BLOCK * MOE_BLOCK
    padded_end = jnp.cumsum(padded)
    padded_start = padded_end - padded
    start = jnp.cumsum(counts) - counts
    rank = jnp.arange(n_assign, dtype=jnp.int32) - start[sorted_e]
    dest = padded_start[sorted_e] + rank
    n_blocks = -(-(n_assign + N_EXPERTS * (MOE_BLOCK - 1)) // MOE_BLOCK)
    n_slots = n_blocks * MOE_BLOCK
    slot_tok = jnp.zeros((n_slots,), jnp.int32).at[dest].set(flat_tok[order])
    slot_w = jnp.zeros((n_slots,), h.dtype).at[dest].set(flat_w[order].astype(h.dtype))
    block_e = jnp.minimum(
        jnp.searchsorted(padded_end, jnp.arange(n_blocks) * MOE_BLOCK, side='right'),
        N_EXPERTS - 1)

    def block_step(out, blk):
        tok, wt, e = blk
        xb = h[tok]
        yb = swiglu(xb, w_gate[e], w_up[e], w_down[e])
        return out.at[tok].add(yb * wt[:, None]), None

    out, _ = lax.scan(block_step, jnp.zeros_like(h),
                      (slot_tok.reshape(n_blocks, MOE_BLOCK),
                       slot_w.reshape(n_blocks, MOE_BLOCK), block_e))
    return out


def setup_inputs(seed: int = 0) -> dict:
    key = jax.random.key(seed)
    ks = jax.random.split(key, 32)
    f32 = jnp.float32

    def nrm(k, shape, scale):
        return jax.random.normal(k, shape, f32) * scale

    G, P, H = N_SSM_GROUPS, SSM_STATE, SSM_GROUP
    n_idx = jnp.arange(P, dtype=f32)
    return {
        'x': nrm(ks[0], (BATCH, SEQ, D_MODEL), 1.0),
        'c': nrm(ks[1], (BATCH, D_MODEL), 1.0),
        'w_ada': nrm(ks[2], (DEPTH, D_MODEL, N_MOD * D_MODEL), 0.5 * D_MODEL ** -0.5),
        'b_ada': nrm(ks[3], (DEPTH, N_MOD * D_MODEL), 0.01),
        'g_mix': 1.0 + nrm(ks[4], (DEPTH, D_MODEL), 0.02),
        'w_in': nrm(ks[5], (DEPTH, D_MODEL, IN_WIDTH), D_MODEL ** -0.5),
        'b_forget': jax.random.uniform(ks[6], (DEPTH, N_ATTN_HEADS), f32, 1.0, 5.0),
        'lam_re': -0.5 + nrm(ks[7], (DEPTH, G, P), 0.01),
        'lam_im': math.pi * n_idx + nrm(ks[8], (DEPTH, G, P), 0.01),
        'log_dt': jax.random.uniform(ks[9], (DEPTH, G), f32, math.log(DT_MIN), math.log(DT_MAX)),
        'b_re': nrm(ks[10], (DEPTH, G, P, H), (2 * H) ** -0.5),
        'b_im': nrm(ks[11], (DEPTH, G, P, H), (2 * H) ** -0.5),
        'c_re': nrm(ks[12], (DEPTH, G, H, P), (2 * P) ** -0.5),
        'c_im': nrm(ks[13], (DEPTH, G, H, P), (2 * P) ** -0.5),
        'd_skip': nrm(ks[14], (DEPTH, G, H), 1.0),
        'w_glu': nrm(ks[15], (DEPTH, SSM_WIDTH, 2 * SSM_WIDTH), SSM_WIDTH ** -0.5),
        'w_proj_attn': nrm(ks[16], (DEPTH, ATTN_WIDTH, D_MODEL), ATTN_WIDTH ** -0.5),
        'w_proj_ssm': nrm(ks[17], (DEPTH, SSM_WIDTH, D_MODEL), SSM_WIDTH ** -0.5),
        'w_out': nrm(ks[18], (DEPTH, D_MODEL, D_MODEL), D_MODEL ** -0.5),
        'g_ffn': 1.0 + nrm(ks[19], (DEPTH, D_MODEL), 0.02),
        'w_router': nrm(ks[20], (DEPTH, D_MODEL, N_EXPERTS), D_MODEL ** -0.5),
        'router_bias': nrm(ks[21], (DEPTH, N_EXPERTS), 0.01),
        'w_gate_e': nrm(ks[22], (DEPTH, N_EXPERTS, D_MODEL, D_EXPERT), D_MODEL ** -0.5),
        'w_up_e': nrm(ks[23], (DEPTH, N_EXPERTS, D_MODEL, D_EXPERT), D_MODEL ** -0.5),
        'w_down_e': nrm(ks[24], (DEPTH, N_EXPERTS, D_EXPERT, D_MODEL), D_EXPERT ** -0.5),
        'w_gate_s': nrm(ks[25], (DEPTH, D_MODEL, D_SHARED), D_MODEL ** -0.5),
        'w_up_s': nrm(ks[26], (DEPTH, D_MODEL, D_SHARED), D_MODEL ** -0.5),
        'w_down_s': nrm(ks[27], (DEPTH, D_SHARED, D_MODEL), D_SHARED ** -0.5),
        'g_final': 1.0 + nrm(ks[28], (D_MODEL,), 0.02),
    }


def reference(x, c, w_ada, b_ada, g_mix, w_in, b_forget, lam_re, lam_im, log_dt,
              b_re, b_im, c_re, c_im, d_skip, w_glu, w_proj_attn, w_proj_ssm, w_out,
              g_ffn, w_router, router_bias, w_gate_e, w_up_e, w_down_e,
              w_gate_s, w_up_s, w_down_s, g_final):
    bsz, seq, d = x.shape
    cond = jax.nn.silu(c)
    for l in range(DEPTH):
        mod = cond @ w_ada[l] + b_ada[l]
        sh1, sc1, gt1, sh2, sc2, gt2 = jnp.split(mod, N_MOD, axis=-1)

        h = modulate(rms_norm(x, g_mix[l]), sh1, sc1)
        proj = jnp.einsum('bsd,de->bse', h, w_in[l])
        q, k, v, f_logit, u, g_a, g_s = jnp.split(proj, IN_SPLITS, axis=-1)
        log_f = jax.nn.log_sigmoid(f_logit.astype(jnp.float32)
                                   + b_forget[l].astype(jnp.float32))
        hd = (bsz, seq, N_ATTN_HEADS, ATTN_HEAD_DIM)
        attn = forgetting_attention(q.reshape(hd), k.reshape(hd), v.reshape(hd), log_f)

        ssm = s5_ssm(u.reshape(bsz, seq, N_SSM_GROUPS, SSM_GROUP), lam_re[l], lam_im[l],
                     log_dt[l], b_re[l], b_im[l], c_re[l], c_im[l], d_skip[l]).astype(x.dtype)
        z_a, z_b = jnp.split(jax.nn.gelu(ssm) @ w_glu[l], 2, axis=-1)
        ssm = z_a * jax.nn.sigmoid(z_b)

        merged = (jax.nn.sigmoid(g_a) * (attn @ w_proj_attn[l])
                  + jax.nn.sigmoid(g_s) * (ssm @ w_proj_ssm[l]))
        x = x + gt1[:, None, :] * (merged @ w_out[l])

        h = modulate(rms_norm(x, g_ffn[l]), sh2, sc2).reshape(bsz * seq, d)
        logits = jnp.einsum('nd,de->ne', h, w_router[l], preferred_element_type=jnp.float32)
        scores = jax.nn.sigmoid(logits)
        _, idx = lax.top_k(scores + router_bias[l].astype(jnp.float32), TOP_K)
        sel = jnp.take_along_axis(scores, idx, axis=-1)
        wts = sel / jnp.sum(sel, axis=-1, keepdims=True) * ROUTED_SCALE
        ffn = (routed_experts(h, idx, wts, w_gate_e[l], w_up_e[l], w_down_e[l])
               + swiglu(h, w_gate_s[l], w_up_s[l], w_down_s[l]))
        x = x + gt2[:, None, :] * ffn.reshape(bsz, seq, d)
    return rms_norm(x, g_final)
```

```python
import functools
import math

import jax
import jax.numpy as jnp
from jax import lax
from jax.experimental import pallas as pl
from jax.experimental.pallas import tpu as pltpu

TOP_K = 8
ROUTED_SCALE = 2.5
NORM_EPS = 1e-6
N_MOD = 6
SSM_SUB = 16
MOE_ROWS = 256

LANES = 128
SUBLANES = 8
VMEM_LIMIT = 56 * 1024 * 1024

F32 = jnp.float32
BF16 = jnp.bfloat16
U32 = jnp.uint32
HI_MASK = 0xFFFF0000


def _tile(n, pref):
    if n <= pref:
        return n
    t = pref
    while n % t:
        t //= 2
    return t


def _params(sem):
    return pltpu.CompilerParams(dimension_semantics=sem, vmem_limit_bytes=VMEM_LIMIT)


def _pack_halves(y):
    w = y.shape[1] // 2
    bits = pltpu.bitcast(y.astype(BF16).astype(F32), U32)
    return bits[:, :w] | (bits[:, w:] >> 16)


def _unpack_halves(p):
    hi = pltpu.bitcast(p & jnp.uint32(HI_MASK), F32)
    lo = pltpu.bitcast(p << 16, F32)
    return hi.astype(BF16), lo.astype(BF16)


def _ada_kernel(c_ref, w_ref, b_ref, o_ref):
    c = c_ref[...]
    cond = c * jax.nn.sigmoid(c)
    o_ref[...] = jnp.dot(cond.astype(BF16), w_ref[...].astype(BF16),
                         preferred_element_type=F32) + b_ref[...]


def _ada(c_pad, w_ada, b_ada):
    rows, d = c_pad.shape
    n = w_ada.shape[1]
    tn = _tile(n, 512)
    return pl.pallas_call(
        _ada_kernel,
        out_shape=jax.ShapeDtypeStruct((rows, n), F32),
        grid=(n // tn,),
        in_specs=[pl.BlockSpec((rows, d), lambda j: (0, 0)),
                  pl.BlockSpec((d, tn), lambda j: (0, j)),
                  pl.BlockSpec((1, tn), lambda j: (0, j))],
        out_specs=pl.BlockSpec((rows, tn), lambda j: (0, j)),
        compiler_params=_params(("arbitrary",)),
        name="ada_mod",
    )(c_pad, w_ada, b_ada.reshape(1, n))


def _norm_mod(x, g_row, mod_ref, shift_row, scale_row):
    ms = jnp.mean(x * x, axis=-1, keepdims=True)
    y = x * lax.rsqrt(ms + NORM_EPS) * g_row
    sc = mod_ref[0, scale_row:scale_row + 1, :]
    sh = mod_ref[0, shift_row:shift_row + 1, :]
    return y * (1.0 + sc) + sh


def _norm_mod_kernel(x_ref, g_ref, mod_ref, o_ref, *, shift_row, scale_row):
    o_ref[...] = _norm_mod(x_ref[...], g_ref[...], mod_ref, shift_row, scale_row).astype(o_ref.dtype)


def _norm_modulate(x2, g, mod3, seq, shift_row, scale_row):
    n, d = x2.shape
    tm = _tile(seq, 256)
    per_b = seq // tm
    return pl.pallas_call(
        functools.partial(_norm_mod_kernel, shift_row=shift_row, scale_row=scale_row),
        out_shape=jax.ShapeDtypeStruct((n, d), BF16),
        grid=(n // tm,),
        in_specs=[pl.BlockSpec((tm, d), lambda i: (i, 0)),
                  pl.BlockSpec((1, d), lambda i: (0, 0)),
                  pl.BlockSpec((1, N_MOD, d), lambda i: (i // per_b, 0, 0))],
        out_specs=pl.BlockSpec((tm, d), lambda i: (i, 0)),
        compiler_params=_params(("arbitrary",)),
        name="norm_mod",
    )(x2, g.reshape(1, d), mod3)


def _mm_kernel(*refs, n_a, a_of_b, epilogue):
    o_ref = refs[-1]
    n_b = len(a_of_b)
    accs = [jnp.dot(refs[a_of_b[p]][...], refs[n_a + p][...], preferred_element_type=F32)
            for p in range(n_b)]
    extras = refs[n_a + n_b:-1]
    o_ref[...] = epilogue(accs, extras).astype(o_ref.dtype)


def _mm(a_list, b_list, extras, epilogue, n_out, out_dtype, name, tm_pref=1024, tn_pref=1024):
    m = a_list[0].shape[0]
    tm = _tile(m, tm_pref)
    tn = _tile(n_out, tn_pref)
    in_specs, args = [], []
    for a in a_list:
        in_specs.append(pl.BlockSpec((tm, a.shape[1]), lambda i, j: (i, 0)))
        args.append(a)
    for _, b, col in b_list:
        in_specs.append(pl.BlockSpec((b.shape[0], tn), lambda i, j, off=col // tn: (0, j + off)))
        args.append(b)
    for arr, bshape, imap in extras:
        in_specs.append(pl.BlockSpec(bshape(tm, tn), imap(tm, tn)))
        args.append(arr)
    return pl.pallas_call(
        functools.partial(_mm_kernel, n_a=len(a_list), a_of_b=tuple(ai for ai, _, _ in b_list),
                          epilogue=epilogue),
        out_shape=jax.ShapeDtypeStruct((m, n_out), out_dtype),
        grid=(m // tm, n_out // tn),
        in_specs=in_specs,
        out_specs=pl.BlockSpec((tm, tn), lambda i, j: (i, j)),
        compiler_params=_params(("arbitrary", "arbitrary")),
        name=name,
    )(*args)


def _row_extra(row):
    return (row, lambda tm, tn: (1, tn), lambda tm, tn: (lambda i, j: (0, j)))


def _tile_extra(arr, col=0):
    return (arr, lambda tm, tn: (tm, tn), lambda tm, tn: (lambda i, j: (i, j + col // tn)))


def _mod_extra(mod3, seq):
    return (mod3, lambda tm, tn: (1, N_MOD, tn),
            lambda tm, tn: (lambda i, j: (i // (seq // tm), 0, j)))


def _cum_kernel(f_ref, b_ref, o_ref, *, chunk):
    seq = f_ref.shape[1]
    x = f_ref[0] + b_ref[...]
    lf = jnp.minimum(x, 0.0) - jnp.log(1.0 + jnp.exp(-jnp.abs(x)))
    o_ref[0] = lf
    r = lax.broadcasted_iota(jnp.int32, (chunk, chunk), 0)
    c = lax.broadcasted_iota(jnp.int32, (chunk, chunk), 1)
    tri = jnp.where(c <= r, 1.0, 0.0).astype(F32)

    def body(i, carry):
        start = pl.multiple_of(i * chunk, chunk)
        blk = o_ref[0, pl.ds(start, chunk), :]
        cs = jnp.dot(tri, blk, precision=lax.Precision.HIGHEST,
                     preferred_element_type=F32) + carry
        o_ref[0, pl.ds(start, chunk), :] = cs
        return cs[chunk - 1:chunk, :]

    lax.fori_loop(0, seq // chunk, body, jnp.zeros((1, f_ref.shape[2]), F32))


def _forget_cumsum(f3, b_row):
    bsz, seq, w = f3.shape
    chunk = _tile(seq, 128)
    return pl.pallas_call(
        functools.partial(_cum_kernel, chunk=chunk),
        out_shape=jax.ShapeDtypeStruct((bsz, seq, w), F32),
        grid=(bsz,),
        in_specs=[pl.BlockSpec((1, seq, w), lambda b: (b, 0, 0)),
                  pl.BlockSpec((1, w), lambda b: (0, 0))],
        out_specs=pl.BlockSpec((1, seq, w), lambda b: (b, 0, 0)),
        compiler_params=_params(("arbitrary",)),
        name="forget_cumsum",
    )(f3, b_row)


def _attn_kernel(c0_ref, q_ref, k_ref, v_ref, ck_ref, o_ref, *, blk, n_heads, n_q):
    b = pl.program_id(0)
    h = pl.program_id(1)
    i = pl.program_id(2)
    c0 = c0_ref[(b * n_heads + h) * n_q + i]
    q = q_ref[0]

    def logits(j):
        start = pl.multiple_of(j * blk, blk)
        kb = k_ref[0, pl.ds(start, blk), :]
        s = lax.dot_general(q, kb, (((1,), (1,)), ((), ())), preferred_element_type=F32)
        return s + (c0 - ck_ref[0, 0, j]), start

    t, start = logits(i)
    row = lax.broadcasted_iota(jnp.int32, (blk, blk), 0)
    col = lax.broadcasted_iota(jnp.int32, (blk, blk), 1)
    t = jnp.where(col <= row, t, -jnp.inf)
    m = jnp.max(t, axis=1, keepdims=True)
    p = jnp.exp(t - m)
    l = jnp.sum(p, axis=1, keepdims=True)
    acc = jnp.dot(p.astype(BF16), v_ref[0, pl.ds(start, blk), :], preferred_element_type=F32)

    def body(j, carry):
        m, l, acc = carry
        t, start = logits(j)
        m_new = jnp.maximum(m, jnp.max(t, axis=1, keepdims=True))
        alpha = jnp.exp(m - m_new)
        p = jnp.exp(t - m_new)
        l = alpha * l + jnp.sum(p, axis=1, keepdims=True)
        acc = alpha * acc + jnp.dot(p.astype(BF16), v_ref[0, pl.ds(start, blk), :],
                                    preferred_element_type=F32)
        return m_new, l, acc

    m, l, acc = lax.fori_loop(0, i, body, (m, l, acc))
    o_ref[0] = (acc / l).astype(o_ref.dtype)


def _attention(qkv3, cum, n_heads, dh):
    bsz, seq, _ = qkv3.shape
    blk = _tile(seq, 512)
    n_q = seq // blk
    cum_t = cum.transpose(0, 2, 1)
    ck = cum_t.reshape(bsz, n_heads, n_q, 1, blk)
    c0 = cum_t[:, :, ::blk].reshape(-1)
    return pl.pallas_call(
        functools.partial(_attn_kernel, blk=blk, n_heads=n_heads, n_q=n_q),
        out_shape=jax.ShapeDtypeStruct((bsz, seq, n_heads * dh), BF16),
        grid_spec=pltpu.PrefetchScalarGridSpec(
            num_scalar_prefetch=1,
            grid=(bsz, n_heads, n_q),
            in_specs=[pl.BlockSpec((1, blk, dh), lambda b, h, i, c: (b, i, h)),
                      pl.BlockSpec((1, seq, dh), lambda b, h, i, c: (b, 0, n_heads + h)),
                      pl.BlockSpec((1, seq, dh), lambda b, h, i, c: (b, 0, 2 * n_heads + h)),
                      pl.BlockSpec((1, 1, n_q, 1, blk), lambda b, h, i, c: (b, h, 0, 0, 0))],
            out_specs=pl.BlockSpec((1, blk, dh), lambda b, h, i, c: (b, i, h)),
        ),
        compiler_params=_params(("arbitrary", "arbitrary", "arbitrary")),
        name="fox_attention",
    )(c0, qkv3, qkv3, qkv3, ck)


def _gelu_tanh(x):
    return 0.5 * x * (1.0 + jnp.tanh(math.sqrt(2.0 / math.pi) * (x + 0.044715 * (x * x * x))))


def _ssm_kernel(u_ref, wcat_ref, wy_ref, apow_ref, dd_ref, o_ref, *, rows_per_seq, n_levels, wide):
    u = u_ref[0]
    yz = jnp.dot(u, wcat_ref[0], preferred_element_type=F32)
    y = yz[:, :wide]
    s = yz[:, wide:]
    n_rows, two_p = s.shape
    pos = lax.broadcasted_iota(jnp.int32, (n_rows, two_p), 0) % rows_per_seq
    half = two_p // 2
    for lvl in range(n_levels):
        shift = 1 << lvl
        a1 = apow_ref[0, 2 * lvl:2 * lvl + 1, :]
        a2 = apow_ref[0, 2 * lvl + 1:2 * lvl + 2, :]
        prev = jnp.where(pos >= shift, pltpu.roll(s, shift, axis=0), 0.0)
        s = s + a1 * prev + a2 * pltpu.roll(prev, half, axis=1)
    s_in = jnp.where(pos >= 1, pltpu.roll(s, 1, axis=0), 0.0)
    y = y + jnp.dot(s_in.astype(BF16), wy_ref[0], preferred_element_type=F32)
    y = y + dd_ref[0] * u.astype(F32)
    o_ref[0] = _gelu_tanh(y).astype(o_ref.dtype)


def _ssm_tables(lam_re, lam_im, log_dt, b_re, b_im, c_re, c_im, d_skip, n_levels):
    g, p = lam_re.shape
    hg = b_re.shape[-1]
    sub = SSM_SUB
    dt = jnp.exp(log_dt)[:, None]
    mag = jnp.exp(lam_re * dt)
    a_re = mag * jnp.cos(lam_im * dt)
    a_im = mag * jnp.sin(lam_im * dt)
    den = lam_re * lam_re + lam_im * lam_im
    k_re = ((a_re - 1) * lam_re + a_im * lam_im) / den
    k_im = (a_im * lam_re - (a_re - 1) * lam_im) / den
    bb_re = k_re[..., None] * b_re - k_im[..., None] * b_im
    bb_im = k_re[..., None] * b_im + k_im[..., None] * b_re

    pr, pi = [jnp.ones_like(a_re)], [jnp.zeros_like(a_im)]
    for _ in range(sub):
        pr.append(pr[-1] * a_re - pi[-1] * a_im)
        pi.append(pr[-2] * a_im + pi[-1] * a_re)
    pw_re = jnp.stack(pr)
    pw_im = jnp.stack(pi)

    ca_re = c_re[None] * pw_re[:sub, :, None, :] - c_im[None] * pw_im[:sub, :, None, :]
    ca_im = c_re[None] * pw_im[:sub, :, None, :] + c_im[None] * pw_re[:sub, :, None, :]
    kk = (jnp.einsum('tgop,gpi->tgoi', ca_re, bb_re, precision=lax.Precision.HIGHEST)
          - jnp.einsum('tgop,gpi->tgoi', ca_im, bb_im, precision=lax.Precision.HIGHEST))
    ii = jnp.arange(sub)
    lag = ii[None, :] - ii[:, None]
    toe = jnp.where((lag >= 0)[:, :, None, None, None], kk[jnp.clip(lag, 0, sub - 1)], 0.0)
    m_tab = toe.transpose(2, 0, 4, 1, 3).reshape(g, sub * hg, sub * hg)

    rev_re = pw_re[:sub][::-1]
    rev_im = pw_im[:sub][::-1]
    z_re = rev_re[..., None] * bb_re[None] - rev_im[..., None] * bb_im[None]
    z_im = rev_re[..., None] * bb_im[None] + rev_im[..., None] * bb_re[None]
    wz = jnp.concatenate([z_re, z_im], axis=2)
    wz = wz.transpose(1, 0, 3, 2).reshape(g, sub * hg, 2 * p)
    wcat = jnp.concatenate([m_tab, wz], axis=2).astype(BF16)

    cb_re = c_re[None] * pw_re[1:, :, None, :] - c_im[None] * pw_im[1:, :, None, :]
    cb_im = c_re[None] * pw_im[1:, :, None, :] + c_im[None] * pw_re[1:, :, None, :]
    wy = jnp.concatenate([cb_re, -cb_im], axis=3)
    wy = wy.transpose(1, 3, 0, 2).reshape(g, 2 * p, sub * hg).astype(BF16)

    rows = []
    lr, li = pw_re[sub], pw_im[sub]
    for _ in range(n_levels):
        rows.append(jnp.concatenate([lr, lr], axis=1))
        rows.append(jnp.concatenate([-li, li], axis=1))
        lr, li = lr * lr - li * li, 2.0 * lr * li
    n_rows = max(8, -(-len(rows) // 8) * 8)
    rows += [jnp.zeros_like(rows[0])] * (n_rows - len(rows))
    apow = jnp.stack(rows, axis=1)
    dd = jnp.tile(d_skip, (1, sub)).reshape(g, 1, sub * hg)
    return wcat, wy, apow, dd


def _s5_ssm(u_bf, seq, lam_re, lam_im, log_dt, b_re, b_im, c_re, c_im, d_skip):
    n, width = u_bf.shape
    g, p = lam_re.shape
    hg = width // g
    sub = SSM_SUB
    rows_per_seq = seq // sub
    n_levels = max(1, (rows_per_seq - 1).bit_length())
    wcat, wy, apow, dd = _ssm_tables(lam_re, lam_im, log_dt, b_re, b_im, c_re, c_im, d_skip, n_levels)
    wide = sub * hg
    r = n // sub
    uf = u_bf.reshape(r, sub, g, hg).transpose(2, 0, 1, 3).reshape(g, r, wide)
    out = pl.pallas_call(
        functools.partial(_ssm_kernel, rows_per_seq=rows_per_seq, n_levels=n_levels, wide=wide),
        out_shape=jax.ShapeDtypeStruct((g, r, wide), BF16),
        grid=(g,),
        in_specs=[pl.BlockSpec((1, r, wide), lambda i: (i, 0, 0)),
                  pl.BlockSpec((1, wide, wide + 2 * p), lambda i: (i, 0, 0)),
                  pl.BlockSpec((1, 2 * p, wide), lambda i: (i, 0, 0)),
                  pl.BlockSpec((1, apow.shape[1], 2 * p), lambda i: (i, 0, 0)),
                  pl.BlockSpec((1, 1, wide), lambda i: (i, 0, 0))],
        out_specs=pl.BlockSpec((1, r, wide), lambda i: (i, 0, 0)),
        compiler_params=_params(("arbitrary",)),
        name="s5_ssm",
    )(uf, wcat, wy, apow, dd)
    return out.reshape(g, r, sub, hg).transpose(1, 2, 0, 3).reshape(n, width)


def _router_kernel(x_ref, g_ref, mod_ref, wr_ref, rb_ref,
                   hp_ref, idx_ref, wts_ref, rank_ref, cnt_ref, carry_ref, *, top_k):
    i = pl.program_id(0)

    @pl.when(i == 0)
    def _():
        carry_ref[...] = jnp.zeros_like(carry_ref)

    h = _norm_mod(x_ref[...], g_ref[...], mod_ref, 3, 4)
    hp_ref[...] = _pack_halves(h)
    h_hi = h.astype(BF16)
    h_lo = (h - h_hi.astype(F32)).astype(BF16)
    nt = (((1,), (1,)), ((), ()))
    logits = (lax.dot_general(wr_ref[0], h_hi, nt, preferred_element_type=F32)
              + lax.dot_general(wr_ref[0], h_lo, nt, preferred_element_type=F32)
              + lax.dot_general(wr_ref[1], h_hi, nt, preferred_element_type=F32))
    scores = jax.nn.sigmoid(logits)
    sel = scores + rb_ref[...]
    n_e, tm = sel.shape
    e_iota = lax.broadcasted_iota(jnp.int32, (n_e, tm), 0)
    hits, vals = [], []
    for _ in range(top_k):
        mx = jnp.max(sel, axis=0, keepdims=True)
        ik = jnp.min(jnp.where(sel == mx, e_iota, n_e), axis=0, keepdims=True)
        hit = e_iota == ik
        vals.append(jnp.sum(jnp.where(hit, scores, 0.0), axis=0, keepdims=True))
        sel = jnp.where(hit, -jnp.inf, sel)
        hits.append(hit)
        idx_ref[pl.ds(len(hits) - 1, 1), :] = ik
    tot = vals[0]
    for v in vals[1:]:
        tot = tot + v
    onehot = jnp.zeros((n_e, tm), F32)
    for k in range(top_k):
        wts_ref[pl.ds(k, 1), :] = vals[k] / tot * ROUTED_SCALE
        onehot = onehot + jnp.where(hits[k], 1.0, 0.0)
    r = lax.broadcasted_iota(jnp.int32, (tm, tm), 0)
    c = lax.broadcasted_iota(jnp.int32, (tm, tm), 1)
    upper = jnp.where(r < c, 1.0, 0.0).astype(BF16)
    base = jnp.dot(onehot.astype(BF16), upper, preferred_element_type=F32) + carry_ref[...]
    for k in range(top_k):
        rk = jnp.sum(jnp.where(hits[k], base, 0.0), axis=0, keepdims=True)
        rank_ref[pl.ds(k, 1), :] = rk.astype(jnp.int32)
    carry_ref[...] = carry_ref[...] + jnp.sum(onehot, axis=1, keepdims=True)
    cnt_ref[...] = carry_ref[...]


def _router(x2, g, mod3, seq, w_router, router_bias):
    n, d = x2.shape
    n_e = w_router.shape[1]
    tm = _tile(seq, 256)
    per_b = seq // tm
    wr_hi = w_router.T.astype(BF16)
    wr_lo = (w_router.T - wr_hi.astype(F32)).astype(BF16)
    wr_t = jnp.stack([wr_hi, wr_lo])
    return pl.pallas_call(
        functools.partial(_router_kernel, top_k=TOP_K),
        out_shape=[jax.ShapeDtypeStruct((n, d // 2), U32),
                   jax.ShapeDtypeStruct((TOP_K, n), jnp.int32),
                   jax.ShapeDtypeStruct((TOP_K, n), F32),
                   jax.ShapeDtypeStruct((TOP_K, n), jnp.int32),
                   jax.ShapeDtypeStruct((n_e, 1), F32)],
        grid=(n // tm,),
        in_specs=[pl.BlockSpec((tm, d), lambda i: (i, 0)),
                  pl.BlockSpec((1, d), lambda i: (0, 0)),
                  pl.BlockSpec((1, N_MOD, d), lambda i: (i // per_b, 0, 0)),
                  pl.BlockSpec((2, n_e, d), lambda i: (0, 0, 0)),
                  pl.BlockSpec((n_e, 1), lambda i: (0, 0))],
        out_specs=[pl.BlockSpec((tm, d // 2), lambda i: (i, 0)),
                   pl.BlockSpec((TOP_K, tm), lambda i: (0, i)),
                   pl.BlockSpec((TOP_K, tm), lambda i: (0, i)),
                   pl.BlockSpec((TOP_K, tm), lambda i: (0, i)),
                   pl.BlockSpec((n_e, 1), lambda i: (0, 0))],
        scratch_shapes=[pltpu.VMEM((n_e, 1), F32)],
        compiler_params=_params(("arbitrary",)),
        name="moe_router",
    )(x2, g.reshape(1, d), mod3, wr_t, router_bias.reshape(n_e, 1))


def _dispatch_kernel(cnt_ref, pstart_ref, nu_ref, dest_ref, h_ref, xs_ref, zero_ref, sem, zsem,
                     *, top_k, n_e, rows):
    i = pl.program_id(0)
    tm = h_ref.shape[0]
    n_blocks = xs_ref.shape[0] // rows

    def issue(r, carry):
        for k in range(top_k):
            pltpu.make_async_copy(h_ref.at[pl.ds(r, 1), :],
                                  xs_ref.at[pl.ds(dest_ref[0, k, r], 1), :], sem).start()
        return carry

    lax.fori_loop(0, tm, issue, 0)

    @pl.when(i == 0)
    def _():
        zero_ref[...] = jnp.zeros_like(zero_ref)

        def zero_fill(wait):
            def go(copy):
                if wait:
                    copy.wait()
                else:
                    copy.start()

            def per_expert(e, carry):
                cnt = cnt_ref[e]
                pad = (rows - cnt % rows) % rows
                first = pstart_ref[e] + cnt

                def one_row(r, c):
                    go(pltpu.make_async_copy(zero_ref.at[pl.ds(0, 1), :],
                                             xs_ref.at[pl.ds(first + r, 1), :], zsem))
                    return c

                lax.fori_loop(0, pad % SUBLANES, one_row, 0)
                end = first + pad
                size = rows // 2
                while size >= SUBLANES:
                    end = end - (pad & size)

                    @pl.when((pad & size) != 0)
                    def _(size=size, end=end):
                        go(pltpu.make_async_copy(zero_ref.at[pl.ds(0, size), :],
                                                 xs_ref.at[pl.ds(pl.multiple_of(end, SUBLANES), size), :],
                                                 zsem))
                    size //= 2
                return carry

            lax.fori_loop(0, n_e, per_expert, 0)

            def per_block(b, carry):
                go(pltpu.make_async_copy(zero_ref, xs_ref.at[pl.ds(pl.multiple_of(b * rows, rows), rows), :],
                                         zsem))
                return carry

            lax.fori_loop(nu_ref[0], n_blocks, per_block, 0)

        zero_fill(False)
        zero_fill(True)

    for k in range(top_k):
        pltpu.make_async_copy(h_ref, xs_ref.at[pl.ds(0, tm), :], sem).wait()


def _dispatch(counts, pstart, n_used, dest3, hp, n_slots):
    n, w = hp.shape
    n_tiles, top_k, tm = dest3.shape
    n_e = counts.shape[0]
    return pl.pallas_call(
        functools.partial(_dispatch_kernel, top_k=top_k, n_e=n_e, rows=MOE_ROWS),
        out_shape=jax.ShapeDtypeStruct((n_slots, w), U32),
        grid_spec=pltpu.PrefetchScalarGridSpec(
            num_scalar_prefetch=3,
            grid=(n_tiles,),
            in_specs=[pl.BlockSpec((1, top_k, tm), lambda i, c, p, u: (i, 0, 0),
                                   memory_space=pltpu.SMEM),
                      pl.BlockSpec((tm, w), lambda i, c, p, u: (i, 0))],
            out_specs=pl.BlockSpec(memory_space=pl.ANY),
            scratch_shapes=[pltpu.VMEM((MOE_ROWS, w), U32), pltpu.SemaphoreType.DMA(()),
                            pltpu.SemaphoreType.DMA(())],
        ),
        compiler_params=_params(("arbitrary",)),
        name="moe_dispatch",
    )(counts, pstart, n_used, dest3, hp)


def _swiglu_packed(xp, wg, wu, wd):
    xh, xl = _unpack_halves(xp)
    half = xh.shape[1]
    gate = (jnp.dot(xh, wg[:half], preferred_element_type=F32)
            + jnp.dot(xl, wg[half:], preferred_element_type=F32))
    up = (jnp.dot(xh, wu[:half], preferred_element_type=F32)
          + jnp.dot(xl, wu[half:], preferred_element_type=F32))
    hm = (gate * jax.nn.sigmoid(gate) * up).astype(BF16)
    return jnp.dot(hm, wd, preferred_element_type=F32)


def _expert_kernel(be_ref, nu_ref, x_ref, wg_ref, wu_ref, wd_ref, o_ref):
    b = pl.program_id(0)

    @pl.when(b < nu_ref[0])
    def _():
        y = _swiglu_packed(x_ref[...], wg_ref[0], wu_ref[0], wd_ref[0])
        o_ref[...] = _pack_halves(y)

    @pl.when(b >= nu_ref[0])
    def _():
        o_ref[...] = jnp.zeros_like(o_ref)


def _experts(block_e, n_used, xs, wg, wu, wd):
    n_slots, w = xs.shape
    n_e, d, de = wg.shape
    rows = MOE_ROWS
    n_blocks = n_slots // rows
    return pl.pallas_call(
        _expert_kernel,
        out_shape=jax.ShapeDtypeStruct((n_slots, w), U32),
        grid_spec=pltpu.PrefetchScalarGridSpec(
            num_scalar_prefetch=2,
            grid=(n_blocks,),
            in_specs=[pl.BlockSpec((rows, w), lambda b, be, nu: (jnp.minimum(b, nu[0] - 1), 0)),
                      pl.BlockSpec((1, d, de), lambda b, be, nu: (be[b], 0, 0)),
                      pl.BlockSpec((1, d, de), lambda b, be, nu: (be[b], 0, 0)),
                      pl.BlockSpec((1, de, d), lambda b, be, nu: (be[b], 0, 0))],
            out_specs=pl.BlockSpec((rows, w), lambda b, be, nu: (b, 0)),
        ),
        compiler_params=_params(("arbitrary",)),
        name="moe_experts",
    )(block_e, n_used, xs, wg, wu, wd)


def _shared_kernel(x_ref, wg_ref, wu_ref, wd_ref, o_ref):
    o_ref[...] = _swiglu_packed(x_ref[...], wg_ref[...], wu_ref[...], wd_ref[...])


def _shared_expert(hp, wg, wu, wd):
    n, w = hp.shape
    d, ds = wg.shape
    tm = _tile(n, 512)
    return pl.pallas_call(
        _shared_kernel,
        out_shape=jax.ShapeDtypeStruct((n, d), F32),
        grid=(n // tm,),
        in_specs=[pl.BlockSpec((tm, w), lambda i: (i, 0)),
                  pl.BlockSpec((d, ds), lambda i: (0, 0)),
                  pl.BlockSpec((d, ds), lambda i: (0, 0)),
                  pl.BlockSpec((ds, d), lambda i: (0, 0))],
        out_specs=pl.BlockSpec((tm, d), lambda i: (i, 0)),
        compiler_params=_params(("arbitrary",)),
        name="moe_shared",
    )(hp, wg, wu, wd)


def _combine_kernel(dest_ref, w_ref, x_ref, sh_ref, mod_ref, g_ref, ys_ref, o_ref, buf_ref, sem, *, top_k):
    tm, d = x_ref.shape
    half = d // 2

    def issue(r, carry):
        for k in range(top_k):
            pltpu.make_async_copy(ys_ref.at[pl.ds(dest_ref[0, k, r], 1), :],
                                  buf_ref.at[k, pl.ds(r, 1), :], sem).start()
        return carry

    lax.fori_loop(0, tm, issue, 0)
    for k in range(top_k):
        pltpu.make_async_copy(ys_ref.at[pl.ds(0, tm), :], buf_ref.at[k], sem).wait()

    acc_h = jnp.zeros((tm, half), F32)
    acc_l = jnp.zeros((tm, half), F32)
    for k in range(top_k):
        p = buf_ref[k]
        wk = w_ref[:, k:k + 1]
        acc_h = acc_h + wk * pltpu.bitcast(p & jnp.uint32(HI_MASK), F32)
        acc_l = acc_l + wk * pltpu.bitcast(p << 16, F32)
    gate = mod_ref[0, 5:6, :]
    x_h = x_ref[:, :half] + gate[:, :half] * (acc_h + sh_ref[:, :half])
    x_l = x_ref[:, half:] + gate[:, half:] * (acc_l + sh_ref[:, half:])
    ms = (jnp.sum(x_h * x_h, axis=1, keepdims=True) + jnp.sum(x_l * x_l, axis=1, keepdims=True)) / d
    inv = lax.rsqrt(ms + NORM_EPS)
    o_ref[:, :half] = x_h * inv * g_ref[:, :half]
    o_ref[:, half:] = x_l * inv * g_ref[:, half:]


def _combine(dest3, wts_t, x2, shared, mod3, g_final, ys, seq):
    n, d = x2.shape
    n_tiles, top_k, tm = dest3.shape
    per_b = seq // tm
    return pl.pallas_call(
        functools.partial(_combine_kernel, top_k=top_k),
        out_shape=jax.ShapeDtypeStruct((n, d), F32),
        grid=(n_tiles,),
        in_specs=[pl.BlockSpec((1, top_k, tm), lambda i: (i, 0, 0), memory_space=pltpu.SMEM),
                  pl.BlockSpec((tm, top_k), lambda i: (i, 0)),
                  pl.BlockSpec((tm, d), lambda i: (i, 0)),
                  pl.BlockSpec((tm, d), lambda i: (i, 0)),
                  pl.BlockSpec((1, N_MOD, d), lambda i: (i // per_b, 0, 0)),
                  pl.BlockSpec((1, d), lambda i: (0, 0)),
                  pl.BlockSpec(memory_space=pl.ANY)],
        out_specs=pl.BlockSpec((tm, d), lambda i: (i, 0)),
        scratch_shapes=[pltpu.VMEM((top_k, tm, d // 2), U32), pltpu.SemaphoreType.DMA(())],
        compiler_params=_params(("arbitrary",)),
        name="moe_combine",
    )(dest3, wts_t, x2, shared, mod3, g_final.reshape(1, d), ys)


def _retile(a, tm):
    k, n = a.shape
    return a.reshape(k, n // tm, tm).transpose(1, 0, 2)


def kernel(x, c, w_ada, b_ada, g_mix, w_in, b_forget, lam_re, lam_im, log_dt, b_re, b_im, c_re, c_im, d_skip, w_glu, w_proj_attn, w_proj_ssm, w_out, g_ffn, w_router, router_bias, w_gate_e, w_up_e, w_down_e, w_gate_s, w_up_s, w_down_s, g_final):
    bsz, seq, d = x.shape
    n = bsz * seq
    depth = w_ada.shape[0]
    n_heads = b_forget.shape[1]
    attn_w = w_proj_attn.shape[1]
    dh = attn_w // n_heads
    ssm_w = w_proj_ssm.shape[1]
    n_e = w_router.shape[2]

    assert depth == 1, "the final rms_norm is fused into the last layer's combine kernel"
    c_pad = jnp.zeros((8, d), F32).at[:bsz].set(c)
    x2 = x.reshape(n, d)
    out = None
    for l in range(depth):
        mod = _ada(c_pad, w_ada[l], b_ada[l])
        mod3 = mod[:bsz].reshape(bsz, N_MOD, d)

        h = _norm_modulate(x2, g_mix[l], mod3, seq, 0, 1)
        w = w_in[l]
        o_f = 3 * attn_w
        o_u = o_f + n_heads
        o_ga = o_u + ssm_w
        w_qkv = w[:, :o_f].astype(BF16)
        q_scale = jnp.concatenate([jnp.full((attn_w,), dh ** -0.5, F32),
                                   jnp.ones((2 * attn_w,), F32)]).reshape(1, o_f)
        qkv = _mm([h], [(0, w_qkv, 0)], [_row_extra(q_scale)],
                  lambda accs, ex: accs[0] * ex[0][...], o_f, BF16, "proj_qkv")
        w_f = jnp.zeros((d, LANES), F32).at[:, :n_heads].set(w[:, o_f:o_u]).astype(BF16)
        f_logit = _mm([h], [(0, w_f, 0)], [], lambda accs, ex: accs[0], LANES, F32, "proj_forget")
        u = _mm([h], [(0, w[:, o_u:o_ga].astype(BF16), 0)], [], lambda accs, ex: accs[0],
                ssm_w, BF16, "proj_u")
        gates = _mm([h], [(0, w[:, o_ga:].astype(BF16), 0)], [],
                    lambda accs, ex: jax.nn.sigmoid(accs[0]), 2 * d, BF16, "proj_gates")

        b_row = jnp.zeros((1, LANES), F32).at[0, :n_heads].set(b_forget[l])
        cum = _forget_cumsum(f_logit.reshape(bsz, seq, LANES), b_row)[:, :, :n_heads]
        attn = _attention(qkv.reshape(bsz, seq, o_f), cum, n_heads, dh).reshape(n, attn_w)

        sg = _s5_ssm(u, seq, lam_re[l], lam_im[l], log_dt[l], b_re[l], b_im[l],
                     c_re[l], c_im[l], d_skip[l])
        w_g = w_glu[l].astype(BF16)
        ssm = _mm([sg], [(0, w_g, 0), (0, w_g, ssm_w)], [],
                  lambda accs, ex: accs[0] * jax.nn.sigmoid(accs[1]), ssm_w, BF16, "ssm_glu")

        merged = _mm([attn, ssm],
                     [(0, w_proj_attn[l].astype(BF16), 0), (1, w_proj_ssm[l].astype(BF16), 0)],
                     [_tile_extra(gates), _tile_extra(gates, d)],
                     lambda accs, ex: (ex[0][...].astype(F32) * accs[0]
                                       + ex[1][...].astype(F32) * accs[1]),
                     d, BF16, "merge_proj")
        x2 = _mm([merged], [(0, w_out[l].astype(BF16), 0)], [_tile_extra(x2), _mod_extra(mod3, seq)],
                 lambda accs, ex: ex[0][...] + ex[1][0, 2:3, :] * accs[0], d, F32, "out_proj")

        hp, idx_t, wts_t, rank_t, cnt = _router(x2, g_ffn[l], mod3, seq, w_router[l], router_bias[l])
        counts = cnt[:, 0].astype(jnp.int32)
        padded = (counts + MOE_ROWS - 1) // MOE_ROWS * MOE_ROWS
        pend = jnp.cumsum(padded)
        pstart = pend - padded
        dest = pstart[idx_t] + rank_t
        n_blocks = -(-(n * TOP_K + n_e * (MOE_ROWS - 1)) // MOE_ROWS)
        n_used = (pend[-1] // MOE_ROWS).astype(jnp.int32).reshape(1)
        blk = jnp.minimum(jnp.arange(n_blocks, dtype=jnp.int32), n_used - 1)
        block_e = jnp.minimum(jnp.searchsorted(pend, blk * MOE_ROWS, side='right'),
                              n_e - 1).astype(jnp.int32)
        tm_d = _tile(seq, 256)
        xs = _dispatch(counts, pstart.astype(jnp.int32), n_used, _retile(dest, tm_d), hp,
                       n_blocks * MOE_ROWS)
        ys = _experts(block_e, n_used, xs, w_gate_e[l].astype(BF16),
                      w_up_e[l].astype(BF16), w_down_e[l].astype(BF16))
        shared = _shared_expert(hp, w_gate_s[l].astype(BF16), w_up_s[l].astype(BF16),
                                w_down_s[l].astype(BF16))
        tm_c = _tile(seq, 128)
        out = _combine(_retile(dest, tm_c), wts_t.T, x2, shared, mod3, g_final, ys, seq)
    return out.reshape(bsz, seq, d)
```

```python
import functools
import math

import jax
import jax.numpy as jnp
from jax import lax
from jax.experimental import pallas as pl
from jax.experimental.pallas import tpu as pltpu

TOP_K = 8
ROUTED_SCALE = 2.5
NORM_EPS = 1e-6
N_MOD = 6
SSM_SUB = 8
SSM_ROWS = 256
MOE_ROWS = 256
ATTN_SUB_ROWS = 512
LOG2E = 1.4426950408889634

LANES = 128
SUBLANES = 8
VMEM_LIMIT = 56 * 1024 * 1024

F32 = jnp.float32
BF16 = jnp.bfloat16
U32 = jnp.uint32
HI_MASK = 0xFFFF0000


def _tile(n, pref):
    if n <= pref:
        return n
    t = pref
    while n % t:
        t //= 2
    return t


def _params(sem):
    return pltpu.CompilerParams(dimension_semantics=sem, vmem_limit_bytes=VMEM_LIMIT)


def _pack_halves(y):
    w = y.shape[1] // 2
    bits = pltpu.bitcast(y.astype(BF16).astype(F32), U32)
    return bits[:, :w] | (bits[:, w:] >> 16)


def _unpack_halves(p):
    hi = pltpu.bitcast(p & jnp.uint32(HI_MASK), F32)
    lo = pltpu.bitcast(p << 16, F32)
    return hi.astype(BF16), lo.astype(BF16)


def _ada_kernel(c_ref, w_ref, b_ref, o_ref):
    c = c_ref[...]
    cond = c * jax.nn.sigmoid(c)
    o_ref[...] = jnp.dot(cond.astype(BF16), w_ref[...].astype(BF16),
                         preferred_element_type=F32) + b_ref[...]


def _ada(c_pad, w_ada, b_ada):
    rows, d = c_pad.shape
    n = w_ada.shape[1]
    tn = _tile(n, 512)
    return pl.pallas_call(
        _ada_kernel,
        out_shape=jax.ShapeDtypeStruct((rows, n), F32),
        grid=(n // tn,),
        in_specs=[pl.BlockSpec((rows, d), lambda j: (0, 0)),
                  pl.BlockSpec((d, tn), lambda j: (0, j)),
                  pl.BlockSpec((1, tn), lambda j: (0, j))],
        out_specs=pl.BlockSpec((rows, tn), lambda j: (0, j)),
        compiler_params=_params(("arbitrary",)),
        name="ada_mod",
    )(c_pad, w_ada, b_ada.reshape(1, n))


def _norm_mod(x, g_row, mod_ref, shift_row, scale_row):
    ms = jnp.mean(x * x, axis=-1, keepdims=True)
    y = x * lax.rsqrt(ms + NORM_EPS) * g_row
    sc = mod_ref[0, scale_row:scale_row + 1, :]
    sh = mod_ref[0, shift_row:shift_row + 1, :]
    return y * (1.0 + sc) + sh


def _norm_mod_kernel(x_ref, g_ref, mod_ref, o_ref, *, shift_row, scale_row):
    o_ref[...] = _norm_mod(x_ref[...], g_ref[...], mod_ref, shift_row, scale_row).astype(o_ref.dtype)


def _norm_modulate(x2, g, mod3, seq, shift_row, scale_row):
    n, d = x2.shape
    tm = _tile(seq, 256)
    per_b = seq // tm
    return pl.pallas_call(
        functools.partial(_norm_mod_kernel, shift_row=shift_row, scale_row=scale_row),
        out_shape=jax.ShapeDtypeStruct((n, d), BF16),
        grid=(n // tm,),
        in_specs=[pl.BlockSpec((tm, d), lambda i: (i, 0)),
                  pl.BlockSpec((1, d), lambda i: (0, 0)),
                  pl.BlockSpec((1, N_MOD, d), lambda i: (i // per_b, 0, 0))],
        out_specs=pl.BlockSpec((tm, d), lambda i: (i, 0)),
        compiler_params=_params(("arbitrary",)),
        name="norm_mod",
    )(x2, g.reshape(1, d), mod3)


def _mm_kernel(*refs, n_a, a_of_b, epilogue):
    o_ref = refs[-1]
    n_b = len(a_of_b)
    lhs = [refs[a][...].astype(BF16) for a in range(n_a)]
    accs = [jnp.dot(lhs[a_of_b[p]], refs[n_a + p][...], preferred_element_type=F32)
            for p in range(n_b)]
    extras = refs[n_a + n_b:-1]
    o_ref[...] = epilogue(accs, extras).astype(o_ref.dtype)


def _mm(a_list, b_list, extras, epilogue, n_out, out_dtype, name, row_unit=None,
        tm_pref=1024, tn_pref=1024):
    m = a_list[0].shape[0]
    tm = _tile(row_unit or m, tm_pref)
    tn = _tile(n_out, tn_pref)
    in_specs, args = [], []
    for a in a_list:
        in_specs.append(pl.BlockSpec((tm, a.shape[1]), lambda i, j: (i, 0)))
        args.append(a)
    for _, b, col in b_list:
        in_specs.append(pl.BlockSpec((b.shape[0], tn), lambda i, j, off=col // tn: (0, j + off)))
        args.append(b)
    for arr, bshape, imap in extras:
        in_specs.append(pl.BlockSpec(bshape(tm, tn), imap(tm, tn)))
        args.append(arr)
    return pl.pallas_call(
        functools.partial(_mm_kernel, n_a=len(a_list), a_of_b=tuple(ai for ai, _, _ in b_list),
                          epilogue=epilogue),
        out_shape=jax.ShapeDtypeStruct((m, n_out), out_dtype),
        grid=(m // tm, n_out // tn),
        in_specs=in_specs,
        out_specs=pl.BlockSpec((tm, tn), lambda i, j: (i, j)),
        compiler_params=_params(("arbitrary", "arbitrary")),
        name=name,
    )(*args)


def _row_extra(row):
    return (row, lambda tm, tn: (1, tn), lambda tm, tn: (lambda i, j: (0, j)))


def _tile_extra(arr, col=0):
    return (arr, lambda tm, tn: (tm, tn), lambda tm, tn: (lambda i, j: (i, j + col // tn)))


def _mod_extra(mod3, seq):
    return (mod3, lambda tm, tn: (1, N_MOD, tn),
            lambda tm, tn: (lambda i, j: (i // (seq // tm), 0, j)))


def _cum_kernel(f_ref, b_ref, o_ref, *, chunk):
    seq = f_ref.shape[1]
    x = f_ref[0] + b_ref[...]
    lf = jnp.minimum(x, 0.0) - jnp.log(1.0 + jnp.exp(-jnp.abs(x)))
    o_ref[0] = lf
    r = lax.broadcasted_iota(jnp.int32, (chunk, chunk), 0)
    c = lax.broadcasted_iota(jnp.int32, (chunk, chunk), 1)
    tri = jnp.where(c <= r, 1.0, 0.0).astype(F32)

    def body(i, carry):
        start = pl.multiple_of(i * chunk, chunk)
        blk = o_ref[0, pl.ds(start, chunk), :]
        cs = jnp.dot(tri, blk, precision=lax.Precision.HIGHEST,
                     preferred_element_type=F32) + carry
        o_ref[0, pl.ds(start, chunk), :] = cs
        return cs[chunk - 1:chunk, :]

    lax.fori_loop(0, seq // chunk, body, jnp.zeros((1, f_ref.shape[2]), F32))


def _forget_cumsum(f3, b_row):
    bsz, seq, w = f3.shape
    chunk = _tile(seq, 128)
    return pl.pallas_call(
        functools.partial(_cum_kernel, chunk=chunk),
        out_shape=jax.ShapeDtypeStruct((bsz, seq, w), F32),
        grid=(bsz,),
        in_specs=[pl.BlockSpec((1, seq, w), lambda b: (b, 0, 0)),
                  pl.BlockSpec((1, w), lambda b: (0, 0))],
        out_specs=pl.BlockSpec((1, seq, w), lambda b: (b, 0, 0)),
        compiler_params=_params(("arbitrary",)),
        name="forget_cumsum",
    )(f3, b_row)


def _attn_kernel(c0_ref, q_ref, k_ref, v_ref, ck_ref, o_ref, sa_ref, sb_ref, *, blk, n_sub, n_heads, n_q):
    b = pl.program_id(0)
    h = pl.program_id(1)
    i = pl.program_id(2)
    c0 = c0_ref[(b * n_heads + h) * n_q + i]
    sub = blk // n_sub
    qs = [q_ref[0, a * sub:(a + 1) * sub, :] for a in range(n_sub)]

    def logits(j, s_ref):
        kb = k_ref[0, pl.ds(pl.multiple_of(j * blk, blk), blk), :]
        bias = (c0 - ck_ref[0, 0, j]) * LOG2E
        for a, q in enumerate(qs):
            s_ref[a * sub:(a + 1) * sub, :] = lax.dot_general(
                q, kb, (((1,), (1,)), ((), ())), preferred_element_type=F32) + bias

    def update(j, s_ref, state, diagonal=False):
        vb = v_ref[0, pl.ds(pl.multiple_of(j * blk, blk), blk), :]
        out = []
        for a, (m, l, acc) in enumerate(state):
            t = s_ref[a * sub:(a + 1) * sub, :]
            if diagonal:
                row = lax.broadcasted_iota(jnp.int32, (sub, blk), 0)
                col = lax.broadcasted_iota(jnp.int32, (sub, blk), 1)
                t = jnp.where(col <= row + a * sub, t, -jnp.inf)
            m_new = jnp.maximum(m, jnp.max(t, axis=1, keepdims=True))
            alpha = jnp.exp2(m - m_new)
            p = jnp.exp2(t - m_new)
            l = alpha * l + jnp.sum(p, axis=1, keepdims=True)
            acc = alpha * acc + jnp.dot(p.astype(BF16), vb, preferred_element_type=F32)
            out.append((m_new, l, acc))
        return tuple(out)

    def body(pair, state):
        j = 2 * pair
        logits(j + 1, sb_ref)
        state = update(j, sa_ref, state)
        logits(j + 2, sa_ref)
        return update(j + 1, sb_ref, state)

    init = tuple((jnp.full((sub, 1), -jnp.inf, F32), jnp.zeros((sub, 1), F32),
                  jnp.zeros((sub, q_ref.shape[2]), F32)) for _ in range(n_sub))
    logits(0, sa_ref)
    state = lax.fori_loop(0, i // 2, body, init)

    def finish(state):
        for a, (m, l, acc) in enumerate(state):
            o_ref[0, a * sub:(a + 1) * sub, :] = (acc / l).astype(o_ref.dtype)

    @pl.when(i % 2 == 0)
    def _():
        finish(update(i, sa_ref, state, diagonal=True))

    @pl.when(i % 2 == 1)
    def _():
        logits(i, sb_ref)
        finish(update(i, sb_ref, update(i - 1, sa_ref, state), diagonal=True))


def _attention(qkv3, cum, n_heads, dh):
    bsz, seq, _ = qkv3.shape
    blk = _tile(seq, 512)
    n_q = seq // blk
    cum_t = cum.transpose(0, 2, 1)
    ck = cum_t.reshape(bsz, n_heads, n_q, 1, blk)
    c0 = cum_t[:, :, ::blk].reshape(-1)
    return pl.pallas_call(
        functools.partial(_attn_kernel, blk=blk, n_sub=max(1, blk // ATTN_SUB_ROWS),
                          n_heads=n_heads, n_q=n_q),
        out_shape=jax.ShapeDtypeStruct((bsz, seq, n_heads * dh), BF16),
        grid_spec=pltpu.PrefetchScalarGridSpec(
            num_scalar_prefetch=1,
            grid=(bsz, n_heads, n_q),
            in_specs=[pl.BlockSpec((1, blk, dh), lambda b, h, i, c: (b, i, h)),
                      pl.BlockSpec((1, seq, dh), lambda b, h, i, c: (b, 0, n_heads + h)),
                      pl.BlockSpec((1, seq, dh), lambda b, h, i, c: (b, 0, 2 * n_heads + h)),
                      pl.BlockSpec((1, 1, n_q, 1, blk), lambda b, h, i, c: (b, h, 0, 0, 0))],
            out_specs=pl.BlockSpec((1, blk, dh), lambda b, h, i, c: (b, i, h)),
            scratch_shapes=[pltpu.VMEM((blk, blk), F32), pltpu.VMEM((blk, blk), F32)],
        ),
        compiler_params=_params(("arbitrary", "arbitrary", "arbitrary")),
        name="fox_attention",
    )(c0, qkv3, qkv3, qkv3, ck)


def _gelu_tanh(x):
    return 0.5 * x * (1.0 + jnp.tanh(math.sqrt(2.0 / math.pi) * (x + 0.044715 * (x * x * x))))


def _ssm_kernel(u_ref, wm_ref, wz_ref, wy_ref, apow_ref, dd_ref, o_ref, xcat_ref, carry_ref,
                *, sub, n_levels):
    rb = u_ref.shape[0] // sub

    @pl.when(pl.program_id(2) == 0)
    def _():
        carry_ref[...] = jnp.zeros_like(carry_ref)

    for i in range(sub):
        xcat_ref[:, i * LANES:(i + 1) * LANES] = u_ref[pl.ds(i, rb, stride=sub), :].astype(BF16)
    x = xcat_ref[...]
    y = jnp.dot(x, wm_ref[0], preferred_element_type=F32)
    s = jnp.dot(x, wz_ref[0], preferred_element_type=F32)
    half = s.shape[1] // 2
    row = lax.broadcasted_iota(jnp.int32, s.shape, 0)
    carry = carry_ref[...]

    def times_a(lvl, v):
        a1 = apow_ref[0, 2 * lvl:2 * lvl + 1, :]
        a2 = apow_ref[0, 2 * lvl + 1:2 * lvl + 2, :]
        return a1 * v + a2 * pltpu.roll(v, half, axis=1)

    s = s + jnp.where(row == 0, times_a(0, carry), 0.0)
    for lvl in range(n_levels):
        shift = 1 << lvl
        s = s + times_a(lvl, jnp.where(row >= shift, pltpu.roll(s, shift, axis=0), 0.0))
    s_in = jnp.where(row >= 1, pltpu.roll(s, 1, axis=0), carry)
    carry_ref[...] = s[rb - 1:rb, :]
    y = y + jnp.dot(s_in.astype(BF16), wy_ref[0], preferred_element_type=F32)
    for j in range(sub):
        yj = y[:, j * LANES:(j + 1) * LANES] + dd_ref[0] * u_ref[pl.ds(j, rb, stride=sub), :]
        o_ref[pl.ds(j, rb, stride=sub), :] = _gelu_tanh(yj)


def _ssm_tables(lam_re, lam_im, log_dt, b_re, b_im, c_re, c_im, d_skip, n_levels):
    g, p = lam_re.shape
    hg = b_re.shape[-1]
    sub = SSM_SUB
    dt = jnp.exp(log_dt)[:, None]
    mag = jnp.exp(lam_re * dt)
    a_re = mag * jnp.cos(lam_im * dt)
    a_im = mag * jnp.sin(lam_im * dt)
    den = lam_re * lam_re + lam_im * lam_im
    k_re = ((a_re - 1) * lam_re + a_im * lam_im) / den
    k_im = (a_im * lam_re - (a_re - 1) * lam_im) / den
    bb_re = k_re[..., None] * b_re - k_im[..., None] * b_im
    bb_im = k_re[..., None] * b_im + k_im[..., None] * b_re

    pr, pi = [jnp.ones_like(a_re)], [jnp.zeros_like(a_im)]
    for _ in range(sub):
        pr.append(pr[-1] * a_re - pi[-1] * a_im)
        pi.append(pr[-2] * a_im + pi[-1] * a_re)
    pw_re = jnp.stack(pr)
    pw_im = jnp.stack(pi)

    ca_re = c_re[None] * pw_re[:sub, :, None, :] - c_im[None] * pw_im[:sub, :, None, :]
    ca_im = c_re[None] * pw_im[:sub, :, None, :] + c_im[None] * pw_re[:sub, :, None, :]
    kk = (jnp.einsum('tgop,gpi->tgoi', ca_re, bb_re, precision=lax.Precision.HIGHEST)
          - jnp.einsum('tgop,gpi->tgoi', ca_im, bb_im, precision=lax.Precision.HIGHEST))
    ii = jnp.arange(sub)
    lag = ii[None, :] - ii[:, None]
    sel = (lag[None] == ii[:, None, None]).astype(F32)
    toe = jnp.sum(sel[:, :, :, None, None, None] * kk[:, None, None], axis=0)

    gt = LANES // hg
    nt = g // gt
    eye = jnp.eye(gt, dtype=F32)
    wide = sub * LANES
    sw = gt * 2 * p
    toe6 = toe.reshape(sub, sub, nt, gt, hg, hg).transpose(2, 0, 3, 5, 1, 4)
    wm = (toe6[:, :, :, :, :, None, :] * eye[None, None, :, None, None, :, None])
    wm = wm.reshape(nt, wide, wide).astype(BF16)

    rev_re = pw_re[:sub][::-1]
    rev_im = pw_im[:sub][::-1]
    z_re = rev_re[..., None] * bb_re[None] - rev_im[..., None] * bb_im[None]
    z_im = rev_re[..., None] * bb_im[None] + rev_im[..., None] * bb_re[None]
    zz = jnp.stack([z_re, z_im]).reshape(2, sub, nt, gt, p, hg).transpose(2, 1, 3, 5, 0, 4)
    wz = (zz[:, :, :, :, :, None, :] * eye[None, None, :, None, None, :, None])
    wz = wz.reshape(nt, wide, sw).astype(BF16)

    cb_re = c_re[None] * pw_re[1:, :, None, :] - c_im[None] * pw_im[1:, :, None, :]
    cb_im = c_re[None] * pw_im[1:, :, None, :] + c_im[None] * pw_re[1:, :, None, :]
    yy = jnp.stack([cb_re, -cb_im]).reshape(2, sub, nt, gt, hg, p).transpose(2, 0, 3, 5, 1, 4)
    wy = (yy[:, :, :, :, :, None, :] * eye[None, None, :, None, None, :, None])
    wy = wy.reshape(nt, sw, wide).astype(BF16)

    rows = []
    lr, li = pw_re[sub].reshape(nt, gt * p), pw_im[sub].reshape(nt, gt * p)
    for _ in range(n_levels):
        rows.append(jnp.concatenate([lr, lr], axis=1))
        rows.append(jnp.concatenate([-li, li], axis=1))
        lr, li = lr * lr - li * li, 2.0 * lr * li
    n_rows = max(SUBLANES, -(-len(rows) // SUBLANES) * SUBLANES)
    rows += [jnp.zeros_like(rows[0])] * (n_rows - len(rows))
    apow = jnp.stack(rows, axis=1)
    dd = d_skip.reshape(nt, 1, LANES)
    return wm, wz, wy, apow, dd


def _s5_ssm(u, seq, lam_re, lam_im, log_dt, b_re, b_im, c_re, c_im, d_skip):
    n, width = u.shape
    g, p = lam_re.shape
    hg = width // g
    sub = SSM_SUB
    assert LANES % hg == 0 and width % LANES == 0
    gt = LANES // hg
    nt = width // LANES
    rb = _tile(seq // sub, SSM_ROWS)
    n_rb = seq // (rb * sub)
    n_levels = max(1, (rb - 1).bit_length())
    wm, wz, wy, apow, dd = _ssm_tables(lam_re, lam_im, log_dt, b_re, b_im, c_re, c_im, d_skip, n_levels)
    wide = sub * LANES
    sw = gt * 2 * p
    return pl.pallas_call(
        functools.partial(_ssm_kernel, sub=sub, n_levels=n_levels),
        out_shape=jax.ShapeDtypeStruct((n, width), F32),
        grid=(nt, n // seq, n_rb),
        in_specs=[pl.BlockSpec((rb * sub, LANES), lambda t, b, k: (b * n_rb + k, t)),
                  pl.BlockSpec((1, wide, wide), lambda t, b, k: (t, 0, 0)),
                  pl.BlockSpec((1, wide, sw), lambda t, b, k: (t, 0, 0)),
                  pl.BlockSpec((1, sw, wide), lambda t, b, k: (t, 0, 0)),
                  pl.BlockSpec((1, apow.shape[1], sw), lambda t, b, k: (t, 0, 0)),
                  pl.BlockSpec((1, 1, LANES), lambda t, b, k: (t, 0, 0))],
        out_specs=pl.BlockSpec((rb * sub, LANES), lambda t, b, k: (b * n_rb + k, t)),
        scratch_shapes=[pltpu.VMEM((rb, wide), BF16), pltpu.VMEM((1, sw), F32)],
        compiler_params=_params(("arbitrary", "arbitrary", "arbitrary")),
        name="s5_ssm",
    )(u, wm, wz, wy, apow, dd)


def _router_kernel(x_ref, g_ref, mod_ref, wr_ref, rb_ref,
                   hp_ref, idx_ref, wts_ref, rank_ref, cnt_ref, carry_ref, *, top_k):
    i = pl.program_id(0)

    @pl.when(i == 0)
    def _():
        carry_ref[...] = jnp.zeros_like(carry_ref)

    h = _norm_mod(x_ref[...], g_ref[...], mod_ref, 3, 4)
    hp_ref[...] = _pack_halves(h)
    h_hi = h.astype(BF16)
    h_lo = (h - h_hi.astype(F32)).astype(BF16)
    nt = (((1,), (1,)), ((), ()))
    logits = (lax.dot_general(wr_ref[0], h_hi, nt, preferred_element_type=F32)
              + lax.dot_general(wr_ref[0], h_lo, nt, preferred_element_type=F32)
              + lax.dot_general(wr_ref[1], h_hi, nt, preferred_element_type=F32))
    scores = jax.nn.sigmoid(logits)
    sel = scores + rb_ref[...]
    n_e, tm = sel.shape
    e_iota = lax.broadcasted_iota(jnp.int32, (n_e, tm), 0)
    hits, vals = [], []
    for _ in range(top_k):
        mx = jnp.max(sel, axis=0, keepdims=True)
        ik = jnp.min(jnp.where(sel == mx, e_iota, n_e), axis=0, keepdims=True)
        hit = e_iota == ik
        vals.append(jnp.sum(jnp.where(hit, scores, 0.0), axis=0, keepdims=True))
        sel = jnp.where(hit, -jnp.inf, sel)
        hits.append(hit)
        idx_ref[pl.ds(len(hits) - 1, 1), :] = ik
    tot = vals[0]
    for v in vals[1:]:
        tot = tot + v
    onehot = jnp.zeros((n_e, tm), F32)
    for k in range(top_k):
        wts_ref[pl.ds(k, 1), :] = vals[k] / tot * ROUTED_SCALE
        onehot = onehot + jnp.where(hits[k], 1.0, 0.0)
    r = lax.broadcasted_iota(jnp.int32, (tm, tm), 0)
    c = lax.broadcasted_iota(jnp.int32, (tm, tm), 1)
    upper = jnp.where(r < c, 1.0, 0.0).astype(BF16)
    base = jnp.dot(onehot.astype(BF16), upper, preferred_element_type=F32) + carry_ref[...]
    for k in range(top_k):
        rk = jnp.sum(jnp.where(hits[k], base, 0.0), axis=0, keepdims=True)
        rank_ref[pl.ds(k, 1), :] = rk.astype(jnp.int32)
    carry_ref[...] = carry_ref[...] + jnp.sum(onehot, axis=1, keepdims=True)
    cnt_ref[...] = carry_ref[...]


def _router(x2, g, mod3, seq, w_router, router_bias):
    n, d = x2.shape
    n_e = w_router.shape[1]
    tm = _tile(seq, 256)
    per_b = seq // tm
    wr_hi = w_router.T.astype(BF16)
    wr_lo = (w_router.T - wr_hi.astype(F32)).astype(BF16)
    wr_t = jnp.stack([wr_hi, wr_lo])
    return pl.pallas_call(
        functools.partial(_router_kernel, top_k=TOP_K),
        out_shape=[jax.ShapeDtypeStruct((n, d // 2), U32),
                   jax.ShapeDtypeStruct((TOP_K, n), jnp.int32),
                   jax.ShapeDtypeStruct((TOP_K, n), F32),
                   jax.ShapeDtypeStruct((TOP_K, n), jnp.int32),
                   jax.ShapeDtypeStruct((n_e, 1), F32)],
        grid=(n // tm,),
        in_specs=[pl.BlockSpec((tm, d), lambda i: (i, 0)),
                  pl.BlockSpec((1, d), lambda i: (0, 0)),
                  pl.BlockSpec((1, N_MOD, d), lambda i: (i // per_b, 0, 0)),
                  pl.BlockSpec((2, n_e, d), lambda i: (0, 0, 0)),
                  pl.BlockSpec((n_e, 1), lambda i: (0, 0))],
        out_specs=[pl.BlockSpec((tm, d // 2), lambda i: (i, 0)),
                   pl.BlockSpec((TOP_K, tm), lambda i: (0, i)),
                   pl.BlockSpec((TOP_K, tm), lambda i: (0, i)),
                   pl.BlockSpec((TOP_K, tm), lambda i: (0, i)),
                   pl.BlockSpec((n_e, 1), lambda i: (0, 0))],
        scratch_shapes=[pltpu.VMEM((n_e, 1), F32)],
        compiler_params=_params(("arbitrary",)),
        name="moe_router",
    )(x2, g.reshape(1, d), mod3, wr_t, router_bias.reshape(n_e, 1))


def _dispatch_kernel(cnt_ref, pstart_ref, nu_ref, dest_ref, h_ref, xs_ref, zero_ref, sem, zsem,
                     *, top_k, n_e, rows):
    i = pl.program_id(0)
    tm = h_ref.shape[0]
    n_blocks = xs_ref.shape[0] // rows

    def issue(r, carry):
        for k in range(top_k):
            pltpu.make_async_copy(h_ref.at[pl.ds(r, 1), :],
                                  xs_ref.at[pl.ds(dest_ref[0, k, r], 1), :], sem).start(priority=k % 2)
        return carry

    lax.fori_loop(0, tm, issue, 0)

    @pl.when(i == 0)
    def _():
        zero_ref[...] = jnp.zeros_like(zero_ref)

        def zero_fill(wait):
            def go(copy):
                if wait:
                    copy.wait()
                else:
                    copy.start()

            def per_expert(e, carry):
                cnt = cnt_ref[e]
                pad = (rows - cnt % rows) % rows
                first = pstart_ref[e] + cnt

                def one_row(r, c):
                    go(pltpu.make_async_copy(zero_ref.at[pl.ds(0, 1), :],
                                             xs_ref.at[pl.ds(first + r, 1), :], zsem))
                    return c

                lax.fori_loop(0, pad % SUBLANES, one_row, 0)
                end = first + pad
                size = rows // 2
                while size >= SUBLANES:
                    end = end - (pad & size)

                    @pl.when((pad & size) != 0)
                    def _(size=size, end=end):
                        go(pltpu.make_async_copy(zero_ref.at[pl.ds(0, size), :],
                                                 xs_ref.at[pl.ds(pl.multiple_of(end, SUBLANES), size), :],
                                                 zsem))
                    size //= 2
                return carry

            lax.fori_loop(0, n_e, per_expert, 0)

            def per_block(b, carry):
                go(pltpu.make_async_copy(zero_ref, xs_ref.at[pl.ds(pl.multiple_of(b * rows, rows), rows), :],
                                         zsem))
                return carry

            lax.fori_loop(nu_ref[0], n_blocks, per_block, 0)

        zero_fill(False)
        zero_fill(True)

    for k in range(top_k):
        pltpu.make_async_copy(h_ref, xs_ref.at[pl.ds(0, tm), :], sem).wait()


def _dispatch(counts, pstart, n_used, dest3, hp, n_slots):
    n, w = hp.shape
    n_tiles, top_k, tm = dest3.shape
    n_e = counts.shape[0]
    return pl.pallas_call(
        functools.partial(_dispatch_kernel, top_k=top_k, n_e=n_e, rows=MOE_ROWS),
        out_shape=jax.ShapeDtypeStruct((n_slots, w), U32),
        grid_spec=pltpu.PrefetchScalarGridSpec(
            num_scalar_prefetch=3,
            grid=(n_tiles,),
            in_specs=[pl.BlockSpec((1, top_k, tm), lambda i, c, p, u: (i, 0, 0),
                                   memory_space=pltpu.SMEM),
                      pl.BlockSpec((tm, w), lambda i, c, p, u: (i, 0))],
            out_specs=pl.BlockSpec(memory_space=pl.ANY),
            scratch_shapes=[pltpu.VMEM((MOE_ROWS, w), U32), pltpu.SemaphoreType.DMA(()),
                            pltpu.SemaphoreType.DMA(())],
        ),
        compiler_params=_params(("arbitrary",)),
        name="moe_dispatch",
    )(counts, pstart, n_used, dest3, hp)


def _swiglu_packed(xp, wg, wu, wd):
    xh, xl = _unpack_halves(xp)
    half = xh.shape[1]
    gate = (jnp.dot(xh, wg[:half], preferred_element_type=F32)
            + jnp.dot(xl, wg[half:], preferred_element_type=F32))
    up = (jnp.dot(xh, wu[:half], preferred_element_type=F32)
          + jnp.dot(xl, wu[half:], preferred_element_type=F32))
    hm = (gate * jax.nn.sigmoid(gate) * up).astype(BF16)
    return jnp.dot(hm, wd, preferred_element_type=F32)


def _expert_kernel(be_ref, nu_ref, x_ref, wg_ref, wu_ref, wd_ref, o_ref):
    b = pl.program_id(0)

    @pl.when(b < nu_ref[0])
    def _():
        y = _swiglu_packed(x_ref[...], wg_ref[0], wu_ref[0], wd_ref[0])
        o_ref[...] = _pack_halves(y)

    @pl.when(b >= nu_ref[0])
    def _():
        o_ref[...] = jnp.zeros_like(o_ref)


def _experts(block_e, n_used, xs, wg, wu, wd):
    n_slots, w = xs.shape
    n_e, d, de = wg.shape
    rows = MOE_ROWS
    n_blocks = n_slots // rows
    return pl.pallas_call(
        _expert_kernel,
        out_shape=jax.ShapeDtypeStruct((n_slots, w), U32),
        grid_spec=pltpu.PrefetchScalarGridSpec(
            num_scalar_prefetch=2,
            grid=(n_blocks,),
            in_specs=[pl.BlockSpec((rows, w), lambda b, be, nu: (jnp.minimum(b, nu[0] - 1), 0)),
                      pl.BlockSpec((1, d, de), lambda b, be, nu: (be[b], 0, 0)),
                      pl.BlockSpec((1, d, de), lambda b, be, nu: (be[b], 0, 0)),
                      pl.BlockSpec((1, de, d), lambda b, be, nu: (be[b], 0, 0))],
            out_specs=pl.BlockSpec((rows, w), lambda b, be, nu: (b, 0)),
        ),
        compiler_params=_params(("arbitrary",)),
        name="moe_experts",
    )(block_e, n_used, xs, wg, wu, wd)


def _shared_kernel(x_ref, wg_ref, wu_ref, wd_ref, o_ref):
    o_ref[...] = _swiglu_packed(x_ref[...], wg_ref[...], wu_ref[...], wd_ref[...])


def _shared_expert(hp, wg, wu, wd):
    n, w = hp.shape
    d, ds = wg.shape
    tm = _tile(n, 512)
    return pl.pallas_call(
        _shared_kernel,
        out_shape=jax.ShapeDtypeStruct((n, d), F32),
        grid=(n // tm,),
        in_specs=[pl.BlockSpec((tm, w), lambda i: (i, 0)),
                  pl.BlockSpec((d, ds), lambda i: (0, 0)),
                  pl.BlockSpec((d, ds), lambda i: (0, 0)),
                  pl.BlockSpec((ds, d), lambda i: (0, 0))],
        out_specs=pl.BlockSpec((tm, d), lambda i: (i, 0)),
        compiler_params=_params(("arbitrary",)),
        name="moe_shared",
    )(hp, wg, wu, wd)


def _combine_kernel(dest_ref, next_ref, w_ref, x_ref, sh_ref, mod_ref, g_ref, ys_ref, o_ref, buf_ref, sems,
                    *, top_k):
    tm, d = x_ref.shape
    half = d // 2
    i = pl.program_id(0)
    slot = i % 2

    def gather(idx_ref, slot):
        def issue(r, carry):
            for k in range(top_k):
                pltpu.make_async_copy(ys_ref.at[pl.ds(idx_ref[0, k, r], 1), :],
                                      buf_ref.at[slot, k, pl.ds(r, 1), :],
                                      sems.at[slot]).start(priority=k % 2)
            return carry
        lax.fori_loop(0, tm, issue, 0)

    @pl.when(i == 0)
    def _():
        gather(dest_ref, 0)

    @pl.when(i + 1 < pl.num_programs(0))
    def _():
        gather(next_ref, 1 - slot)

    for k in range(top_k):
        pltpu.make_async_copy(ys_ref.at[pl.ds(0, tm), :], buf_ref.at[slot, k], sems.at[slot]).wait()

    acc_h = jnp.zeros((tm, half), F32)
    acc_l = jnp.zeros((tm, half), F32)
    for k in range(top_k):
        p = buf_ref[slot, k]
        wk = w_ref[:, k:k + 1]
        acc_h = acc_h + wk * pltpu.bitcast(p & jnp.uint32(HI_MASK), F32)
        acc_l = acc_l + wk * pltpu.bitcast(p << 16, F32)
    gate = mod_ref[0, 5:6, :]
    x_h = x_ref[:, :half] + gate[:, :half] * (acc_h + sh_ref[:, :half])
    x_l = x_ref[:, half:] + gate[:, half:] * (acc_l + sh_ref[:, half:])
    ms = (jnp.sum(x_h * x_h, axis=1, keepdims=True) + jnp.sum(x_l * x_l, axis=1, keepdims=True)) / d
    inv = lax.rsqrt(ms + NORM_EPS)
    o_ref[:, :half] = x_h * inv * g_ref[:, :half]
    o_ref[:, half:] = x_l * inv * g_ref[:, half:]


def _combine(dest3, wts_t, x2, shared, mod3, g_final, ys, seq):
    n, d = x2.shape
    n_tiles, top_k, tm = dest3.shape
    per_b = seq // tm
    return pl.pallas_call(
        functools.partial(_combine_kernel, top_k=top_k),
        out_shape=jax.ShapeDtypeStruct((n, d), F32),
        grid=(n_tiles,),
        in_specs=[pl.BlockSpec((1, top_k, tm), lambda i: (i, 0, 0), memory_space=pltpu.SMEM),
                  pl.BlockSpec((1, top_k, tm), lambda i: (jnp.minimum(i + 1, n_tiles - 1), 0, 0),
                               memory_space=pltpu.SMEM),
                  pl.BlockSpec((tm, top_k), lambda i: (i, 0)),
                  pl.BlockSpec((tm, d), lambda i: (i, 0)),
                  pl.BlockSpec((tm, d), lambda i: (i, 0)),
                  pl.BlockSpec((1, N_MOD, d), lambda i: (i // per_b, 0, 0)),
                  pl.BlockSpec((1, d), lambda i: (0, 0)),
                  pl.BlockSpec(memory_space=pl.ANY)],
        out_specs=pl.BlockSpec((tm, d), lambda i: (i, 0)),
        scratch_shapes=[pltpu.VMEM((2, top_k, tm, d // 2), U32), pltpu.SemaphoreType.DMA((2,))],
        compiler_params=_params(("arbitrary",)),
        name="moe_combine",
    )(dest3, dest3, wts_t, x2, shared, mod3, g_final.reshape(1, d), ys)


def _retile(a, tm):
    k, n = a.shape
    return a.reshape(k, n // tm, tm).transpose(1, 0, 2)


def kernel(x, c, w_ada, b_ada, g_mix, w_in, b_forget, lam_re, lam_im, log_dt, b_re, b_im, c_re, c_im, d_skip, w_glu, w_proj_attn, w_proj_ssm, w_out, g_ffn, w_router, router_bias, w_gate_e, w_up_e, w_down_e, w_gate_s, w_up_s, w_down_s, g_final):
    bsz, seq, d = x.shape
    n = bsz * seq
    depth = w_ada.shape[0]
    n_heads = b_forget.shape[1]
    attn_w = w_proj_attn.shape[1]
    dh = attn_w // n_heads
    ssm_w = w_proj_ssm.shape[1]
    n_e = w_router.shape[2]

    assert depth == 1, "the final rms_norm is fused into the last layer's combine kernel"
    c_pad = jnp.zeros((8, d), F32).at[:bsz].set(c)
    x2 = x.reshape(n, d)
    out = None
    for l in range(depth):
        mod = _ada(c_pad, w_ada[l], b_ada[l])
        mod3 = mod[:bsz].reshape(bsz, N_MOD, d)

        h = _norm_modulate(x2, g_mix[l], mod3, seq, 0, 1)
        w = w_in[l]
        o_f = 3 * attn_w
        o_u = o_f + n_heads
        o_ga = o_u + ssm_w
        w_qkv = w[:, :o_f].astype(BF16)
        q_scale = jnp.concatenate([jnp.full((attn_w,), dh ** -0.5 * LOG2E, F32),
                                   jnp.ones((2 * attn_w,), F32)]).reshape(1, o_f)
        qkv = _mm([h], [(0, w_qkv, 0)], [_row_extra(q_scale)],
                  lambda accs, ex: accs[0] * ex[0][...], o_f, BF16, "proj_qkv")
        w_f = jnp.zeros((d, LANES), F32).at[:, :n_heads].set(w[:, o_f:o_u]).astype(BF16)
        f_logit = _mm([h], [(0, w_f, 0)], [], lambda accs, ex: accs[0], LANES, F32, "proj_forget")
        u = _mm([h], [(0, w[:, o_u:o_ga].astype(BF16), 0)], [], lambda accs, ex: accs[0],
                ssm_w, F32, "proj_u")
        gates = _mm([h], [(0, w[:, o_ga:].astype(BF16), 0)], [],
                    lambda accs, ex: jax.nn.sigmoid(accs[0]), 2 * d, BF16, "proj_gates")

        b_row = jnp.zeros((1, LANES), F32).at[0, :n_heads].set(b_forget[l])
        cum = _forget_cumsum(f_logit.reshape(bsz, seq, LANES), b_row)[:, :, :n_heads]
        attn = _attention(qkv.reshape(bsz, seq, o_f), cum, n_heads, dh).reshape(n, attn_w)

        sg = _s5_ssm(u, seq, lam_re[l], lam_im[l], log_dt[l], b_re[l], b_im[l],
                     c_re[l], c_im[l], d_skip[l])
        w_g = w_glu[l].astype(BF16)
        ssm = _mm([sg], [(0, w_g, 0), (0, w_g, ssm_w)], [],
                  lambda accs, ex: accs[0] * jax.nn.sigmoid(accs[1]), ssm_w, BF16, "ssm_glu")

        merged = _mm([attn, ssm],
                     [(0, w_proj_attn[l].astype(BF16), 0), (1, w_proj_ssm[l].astype(BF16), 0)],
                     [_tile_extra(gates), _tile_extra(gates, d)],
                     lambda accs, ex: (ex[0][...].astype(F32) * accs[0]
                                       + ex[1][...].astype(F32) * accs[1]),
                     d, BF16, "merge_proj")
        x2 = _mm([merged], [(0, w_out[l].astype(BF16), 0)], [_tile_extra(x2), _mod_extra(mod3, seq)],
                 lambda accs, ex: ex[0][...] + ex[1][0, 2:3, :] * accs[0], d, F32, "out_proj",
                 row_unit=seq)

        hp, idx_t, wts_t, rank_t, cnt = _router(x2, g_ffn[l], mod3, seq, w_router[l], router_bias[l])
        counts = cnt[:, 0].astype(jnp.int32)
        padded = (counts + MOE_ROWS - 1) // MOE_ROWS * MOE_ROWS
        pend = jnp.cumsum(padded)
        pstart = pend - padded
        e_ids = jnp.arange(n_e, dtype=jnp.int32)
        first = jnp.sum(jnp.where(idx_t[None] == e_ids[:, None, None], pstart[:, None, None], 0), axis=0)
        dest = first + rank_t
        n_blocks = -(-(n * TOP_K + n_e * (MOE_ROWS - 1)) // MOE_ROWS)
        n_used = (pend[-1] // MOE_ROWS).astype(jnp.int32).reshape(1)
        blk = jnp.minimum(jnp.arange(n_blocks, dtype=jnp.int32), n_used - 1)
        block_e = jnp.minimum(jnp.sum((blk * MOE_ROWS)[:, None] >= pend[None, :], axis=1),
                              n_e - 1).astype(jnp.int32)
        tm_d = _tile(seq, 256)
        xs = _dispatch(counts, pstart.astype(jnp.int32), n_used, _retile(dest, tm_d), hp,
                       n_blocks * MOE_ROWS)
        ys = _experts(block_e, n_used, xs, w_gate_e[l].astype(BF16),
                      w_up_e[l].astype(BF16), w_down_e[l].astype(BF16))
        shared = _shared_expert(hp, w_gate_s[l].astype(BF16), w_up_s[l].astype(BF16),
                                w_down_s[l].astype(BF16))
        tm_c = _tile(seq, 128)
        out = _combine(_retile(dest, tm_c), wts_t.T, x2, shared, mod3, g_final, ys, seq)
    return out.reshape(bsz, seq, d)
```

```python
import functools
import math

import jax
import jax.numpy as jnp
from jax import lax
from jax.experimental import pallas as pl
from jax.experimental.pallas import tpu as pltpu

TOP_K = 8
ROUTED_SCALE = 2.5
NORM_EPS = 1e-6
N_MOD = 6
SSM_SUB = 8
SSM_ROWS = 256
MOE_ROWS = 256
ATTN_SUB_ROWS = 512
LOG2E = 1.4426950408889634

LANES = 128
SUBLANES = 8
VMEM_LIMIT = 56 * 1024 * 1024

F32 = jnp.float32
BF16 = jnp.bfloat16
U32 = jnp.uint32
HI_MASK = 0xFFFF0000


def _tile(n, pref):
    if n <= pref:
        return n
    t = pref
    while n % t:
        t //= 2
    return t


def _params(sem):
    return pltpu.CompilerParams(dimension_semantics=sem, vmem_limit_bytes=VMEM_LIMIT)


def _pack_halves(y):
    w = y.shape[1] // 2
    bits = pltpu.bitcast(y.astype(BF16).astype(F32), U32)
    return bits[:, :w] | (bits[:, w:] >> 16)


def _unpack_halves(p):
    hi = pltpu.bitcast(p & jnp.uint32(HI_MASK), F32)
    lo = pltpu.bitcast(p << 16, F32)
    return hi.astype(BF16), lo.astype(BF16)


def _ada_kernel(c_ref, w_ref, b_ref, o_ref):
    c = c_ref[...]
    cond = c * jax.nn.sigmoid(c)
    o_ref[...] = jnp.dot(cond.astype(BF16), w_ref[...].astype(BF16),
                         preferred_element_type=F32) + b_ref[...]


def _ada(c_pad, w_ada, b_ada):
    rows, d = c_pad.shape
    n = w_ada.shape[1]
    tn = _tile(n, 512)
    return pl.pallas_call(
        _ada_kernel,
        out_shape=jax.ShapeDtypeStruct((rows, n), F32),
        grid=(n // tn,),
        in_specs=[pl.BlockSpec((rows, d), lambda j: (0, 0)),
                  pl.BlockSpec((d, tn), lambda j: (0, j)),
                  pl.BlockSpec((1, tn), lambda j: (0, j))],
        out_specs=pl.BlockSpec((rows, tn), lambda j: (0, j)),
        compiler_params=_params(("arbitrary",)),
        name="ada_mod",
    )(c_pad, w_ada, b_ada.reshape(1, n))


def _norm_mod(x, g_row, mod_ref, shift_row, scale_row):
    ms = jnp.mean(x * x, axis=-1, keepdims=True)
    y = x * lax.rsqrt(ms + NORM_EPS) * g_row
    sc = mod_ref[0, scale_row:scale_row + 1, :]
    sh = mod_ref[0, shift_row:shift_row + 1, :]
    return y * (1.0 + sc) + sh


def _norm_mod_kernel(x_ref, g_ref, mod_ref, o_ref, *, shift_row, scale_row):
    o_ref[...] = _norm_mod(x_ref[...], g_ref[...], mod_ref, shift_row, scale_row).astype(o_ref.dtype)


def _norm_modulate(x2, g, mod3, seq, shift_row, scale_row):
    n, d = x2.shape
    tm = _tile(seq, 256)
    per_b = seq // tm
    return pl.pallas_call(
        functools.partial(_norm_mod_kernel, shift_row=shift_row, scale_row=scale_row),
        out_shape=jax.ShapeDtypeStruct((n, d), BF16),
        grid=(n // tm,),
        in_specs=[pl.BlockSpec((tm, d), lambda i: (i, 0)),
                  pl.BlockSpec((1, d), lambda i: (0, 0)),
                  pl.BlockSpec((1, N_MOD, d), lambda i: (i // per_b, 0, 0))],
        out_specs=pl.BlockSpec((tm, d), lambda i: (i, 0)),
        compiler_params=_params(("arbitrary",)),
        name="norm_mod",
    )(x2, g.reshape(1, d), mod3)


def _mm_kernel(*refs, n_a, a_of_b, epilogue):
    o_ref = refs[-1]
    n_b = len(a_of_b)
    lhs = [refs[a][...].astype(BF16) for a in range(n_a)]
    accs = [jnp.dot(lhs[a_of_b[p]], refs[n_a + p][...], preferred_element_type=F32)
            for p in range(n_b)]
    extras = refs[n_a + n_b:-1]
    o_ref[...] = epilogue(accs, extras).astype(o_ref.dtype)


def _mm(a_list, b_list, extras, epilogue, n_out, out_dtype, name, row_unit=None,
        tm_pref=1024, tn_pref=1024):
    m = a_list[0].shape[0]
    tm = _tile(row_unit or m, tm_pref)
    tn = _tile(n_out, tn_pref)
    in_specs, args = [], []
    for a in a_list:
        in_specs.append(pl.BlockSpec((tm, a.shape[1]), lambda i, j: (i, 0)))
        args.append(a)
    for _, b, col in b_list:
        in_specs.append(pl.BlockSpec((b.shape[0], tn), lambda i, j, off=col // tn: (0, j + off)))
        args.append(b)
    for arr, bshape, imap in extras:
        in_specs.append(pl.BlockSpec(bshape(tm, tn), imap(tm, tn)))
        args.append(arr)
    return pl.pallas_call(
        functools.partial(_mm_kernel, n_a=len(a_list), a_of_b=tuple(ai for ai, _, _ in b_list),
                          epilogue=epilogue),
        out_shape=jax.ShapeDtypeStruct((m, n_out), out_dtype),
        grid=(m // tm, n_out // tn),
        in_specs=in_specs,
        out_specs=pl.BlockSpec((tm, tn), lambda i, j: (i, j)),
        compiler_params=_params(("arbitrary", "arbitrary")),
        name=name,
    )(*args)


def _row_extra(row):
    return (row, lambda tm, tn: (1, tn), lambda tm, tn: (lambda i, j: (0, j)))


def _tile_extra(arr, col=0):
    return (arr, lambda tm, tn: (tm, tn), lambda tm, tn: (lambda i, j: (i, j + col // tn)))


def _mod_extra(mod3, seq):
    return (mod3, lambda tm, tn: (1, N_MOD, tn),
            lambda tm, tn: (lambda i, j: (i // (seq // tm), 0, j)))


def _cum_kernel(f_ref, b_ref, o_ref, *, chunk):
    seq = f_ref.shape[1]
    x = f_ref[0] + b_ref[...]
    lf = jnp.minimum(x, 0.0) - jnp.log(1.0 + jnp.exp(-jnp.abs(x)))
    o_ref[0] = lf
    r = lax.broadcasted_iota(jnp.int32, (chunk, chunk), 0)
    c = lax.broadcasted_iota(jnp.int32, (chunk, chunk), 1)
    tri = jnp.where(c <= r, 1.0, 0.0).astype(F32)

    def body(i, carry):
        start = pl.multiple_of(i * chunk, chunk)
        blk = o_ref[0, pl.ds(start, chunk), :]
        cs = jnp.dot(tri, blk, precision=lax.Precision.HIGHEST,
                     preferred_element_type=F32) + carry
        o_ref[0, pl.ds(start, chunk), :] = cs
        return cs[chunk - 1:chunk, :]

    lax.fori_loop(0, seq // chunk, body, jnp.zeros((1, f_ref.shape[2]), F32))


def _forget_cumsum(f3, b_row):
    bsz, seq, w = f3.shape
    chunk = _tile(seq, 128)
    return pl.pallas_call(
        functools.partial(_cum_kernel, chunk=chunk),
        out_shape=jax.ShapeDtypeStruct((bsz, seq, w), F32),
        grid=(bsz,),
        in_specs=[pl.BlockSpec((1, seq, w), lambda b: (b, 0, 0)),
                  pl.BlockSpec((1, w), lambda b: (0, 0))],
        out_specs=pl.BlockSpec((1, seq, w), lambda b: (b, 0, 0)),
        compiler_params=_params(("arbitrary",)),
        name="forget_cumsum",
    )(f3, b_row)


def _attn_kernel(c0_ref, q_ref, k_ref, v_ref, ck_ref, *refs, blk, n_sub, n_heads, n_q, n_side):
    side_in = refs[:n_side]
    o_ref = refs[n_side]
    side_out = refs[n_side + 1:2 * n_side + 1]
    sa_ref, sb_ref = refs[2 * n_side + 1:]
    b = pl.program_id(0)
    h = pl.program_id(1)
    i = pl.program_id(2)
    c0 = c0_ref[(b * n_heads + h) * n_q + i]
    for w_ref, wo_ref in zip(side_in, side_out):
        wo_ref[...] = w_ref[...].astype(wo_ref.dtype)
    sub = blk // n_sub
    qs = [q_ref[0, a * sub:(a + 1) * sub, :] for a in range(n_sub)]

    def logits(j, s_ref):
        kb = k_ref[0, pl.ds(pl.multiple_of(j * blk, blk), blk), :]
        bias = (c0 - ck_ref[0, 0, j]) * LOG2E
        for a, q in enumerate(qs):
            s_ref[a * sub:(a + 1) * sub, :] = lax.dot_general(
                q, kb, (((1,), (1,)), ((), ())), preferred_element_type=F32) + bias

    def update(j, s_ref, state, diagonal=False):
        vb = v_ref[0, pl.ds(pl.multiple_of(j * blk, blk), blk), :]
        out = []
        for a, (m, l, acc) in enumerate(state):
            t = s_ref[a * sub:(a + 1) * sub, :]
            if diagonal:
                row = lax.broadcasted_iota(jnp.int32, (sub, blk), 0)
                col = lax.broadcasted_iota(jnp.int32, (sub, blk), 1)
                t = jnp.where(col <= row + a * sub, t, -jnp.inf)
            m_new = jnp.maximum(m, jnp.max(t, axis=1, keepdims=True))
            alpha = jnp.exp2(m - m_new)
            p = jnp.exp2(t - m_new)
            l = alpha * l + jnp.sum(p, axis=1, keepdims=True)
            acc = alpha * acc + jnp.dot(p.astype(BF16), vb, preferred_element_type=F32)
            out.append((m_new, l, acc))
        return tuple(out)

    def body(pair, state):
        j = 2 * pair
        logits(j + 1, sb_ref)
        state = update(j, sa_ref, state)
        logits(j + 2, sa_ref)
        return update(j + 1, sb_ref, state)

    init = tuple((jnp.full((sub, 1), -jnp.inf, F32), jnp.zeros((sub, 1), F32),
                  jnp.zeros((sub, q_ref.shape[2]), F32)) for _ in range(n_sub))
    logits(0, sa_ref)
    state = lax.fori_loop(0, i // 2, body, init)

    def finish(state):
        for a, (m, l, acc) in enumerate(state):
            o_ref[0, a * sub:(a + 1) * sub, :] = (acc / l).astype(o_ref.dtype)

    @pl.when(i % 2 == 0)
    def _():
        finish(update(i, sa_ref, state, diagonal=True))

    @pl.when(i % 2 == 1)
    def _():
        logits(i, sb_ref)
        finish(update(i, sb_ref, update(i - 1, sa_ref, state), diagonal=True))


def _attention(qkv3, cum, n_heads, dh, side=()):
    bsz, seq, _ = qkv3.shape
    blk = _tile(seq, 512)
    n_q = seq // blk
    n_steps = bsz * n_heads * n_q
    cum_t = cum.transpose(0, 2, 1)
    ck = cum_t.reshape(bsz, n_heads, n_q, 1, blk)
    c0 = cum_t[:, :, ::blk].reshape(-1)
    side2 = [w.reshape(-1, w.shape[-1]) for w in side]
    bf16_rows = 2 * SUBLANES
    assert all(w.shape[0] % (n_steps * bf16_rows) == 0 for w in side2)

    def slab(w):
        return pl.BlockSpec((w.shape[0] // n_steps, w.shape[1]),
                            lambda b, h, i, c: ((b * n_heads + h) * n_q + i, 0))

    out = pl.pallas_call(
        functools.partial(_attn_kernel, blk=blk, n_sub=max(1, blk // ATTN_SUB_ROWS),
                          n_heads=n_heads, n_q=n_q, n_side=len(side2)),
        out_shape=[jax.ShapeDtypeStruct((bsz, seq, n_heads * dh), BF16)]
        + [jax.ShapeDtypeStruct(w.shape, BF16) for w in side2],
        grid_spec=pltpu.PrefetchScalarGridSpec(
            num_scalar_prefetch=1,
            grid=(bsz, n_heads, n_q),
            in_specs=[pl.BlockSpec((1, blk, dh), lambda b, h, i, c: (b, i, h)),
                      pl.BlockSpec((1, seq, dh), lambda b, h, i, c: (b, 0, n_heads + h)),
                      pl.BlockSpec((1, seq, dh), lambda b, h, i, c: (b, 0, 2 * n_heads + h)),
                      pl.BlockSpec((1, 1, n_q, 1, blk), lambda b, h, i, c: (b, h, 0, 0, 0))]
            + [slab(w) for w in side2],
            out_specs=[pl.BlockSpec((1, blk, dh), lambda b, h, i, c: (b, i, h))]
            + [slab(w) for w in side2],
            scratch_shapes=[pltpu.VMEM((blk, blk), F32), pltpu.VMEM((blk, blk), F32)],
        ),
        compiler_params=_params(("arbitrary", "arbitrary", "arbitrary")),
        name="fox_attention",
    )(c0, qkv3, qkv3, qkv3, ck, *side2)
    return out[0], [o.reshape(w.shape) for o, w in zip(out[1:], side)]


def _gelu_tanh(x):
    return 0.5 * x * (1.0 + jnp.tanh(math.sqrt(2.0 / math.pi) * (x + 0.044715 * (x * x * x))))


def _ssm_kernel(u_ref, wm_ref, wz_ref, wy_ref, apow_ref, dd_ref, o_ref, xcat_ref, carry_ref,
                *, sub, n_levels):
    rb = u_ref.shape[0] // sub

    @pl.when(pl.program_id(2) == 0)
    def _():
        carry_ref[...] = jnp.zeros_like(carry_ref)

    for i in range(sub):
        xcat_ref[:, i * LANES:(i + 1) * LANES] = u_ref[pl.ds(i, rb, stride=sub), :].astype(BF16)
    x = xcat_ref[...]
    y = jnp.dot(x, wm_ref[0], preferred_element_type=F32)
    s = jnp.dot(x, wz_ref[0], preferred_element_type=F32)
    half = s.shape[1] // 2
    row = lax.broadcasted_iota(jnp.int32, s.shape, 0)
    carry = carry_ref[...]

    a1 = apow_ref[0, 0:1, :]
    a2 = apow_ref[0, 1:2, :]

    def times_a(v):
        return a1 * v + a2 * pltpu.roll(v, half, axis=1)

    s = s + jnp.where(row == 0, times_a(carry), 0.0)
    for lvl in range(n_levels):
        shift = 1 << lvl
        s = s + times_a(jnp.where(row >= shift, pltpu.roll(s, shift, axis=0), 0.0))
        a1, a2 = a1 * a1 - a2 * a2, 2.0 * a1 * a2
    s_in = jnp.where(row >= 1, pltpu.roll(s, 1, axis=0), carry)
    carry_ref[...] = s[rb - 1:rb, :]
    y = y + jnp.dot(s_in.astype(BF16), wy_ref[0], preferred_element_type=F32)
    for j in range(sub):
        yj = y[:, j * LANES:(j + 1) * LANES] + dd_ref[0] * u_ref[pl.ds(j, rb, stride=sub), :]
        o_ref[pl.ds(j, rb, stride=sub), :] = _gelu_tanh(yj)


def _ssm_tables(lam_re, lam_im, log_dt, b_re, b_im, c_re, c_im, d_skip):
    g, p = lam_re.shape
    hg = b_re.shape[-1]
    sub = SSM_SUB
    dt = jnp.exp(log_dt)[:, None]
    mag = jnp.exp(lam_re * dt)
    a_re = mag * jnp.cos(lam_im * dt)
    a_im = mag * jnp.sin(lam_im * dt)
    den = lam_re * lam_re + lam_im * lam_im
    k_re = ((a_re - 1) * lam_re + a_im * lam_im) / den
    k_im = (a_im * lam_re - (a_re - 1) * lam_im) / den
    bb_re = k_re[..., None] * b_re - k_im[..., None] * b_im
    bb_im = k_re[..., None] * b_im + k_im[..., None] * b_re

    tau = jnp.arange(sub + 1, dtype=F32)[:, None, None]
    pmag = jnp.exp(tau * (lam_re * dt))
    pw_re = pmag * jnp.cos(tau * (lam_im * dt))
    pw_im = pmag * jnp.sin(tau * (lam_im * dt))

    ca_re = c_re[None] * pw_re[:sub, :, None, :] - c_im[None] * pw_im[:sub, :, None, :]
    ca_im = c_re[None] * pw_im[:sub, :, None, :] + c_im[None] * pw_re[:sub, :, None, :]
    kk = (jnp.einsum('tgop,gpi->tgio', ca_re, bb_re, precision=lax.Precision.HIGHEST)
          - jnp.einsum('tgop,gpi->tgio', ca_im, bb_im, precision=lax.Precision.HIGHEST))

    gt = LANES // hg
    nt = g // gt
    wide = sub * LANES
    sw = gt * 2 * p

    def block_diag(x, rows_per_group, cols_per_group):
        n_r, n_c = gt * rows_per_group, gt * cols_per_group
        r = lax.broadcasted_iota(jnp.int32, (n_r, n_c), 0) // rows_per_group
        c = lax.broadcasted_iota(jnp.int32, (n_r, n_c), 1) // cols_per_group
        return jnp.where(r == c, jnp.tile(x, (1,) * (x.ndim - 1) + (gt,)), 0.0)

    bd = block_diag(kk.reshape(sub, nt, LANES, hg), hg, hg).astype(BF16)
    zero = jnp.zeros_like(bd[0])
    wm = jnp.stack([jnp.concatenate([zero] * i + [bd[k] for k in range(sub - i)], axis=-1)
                    for i in range(sub)], axis=1).reshape(nt, wide, wide)

    rev_re = pw_re[:sub][::-1]
    rev_im = pw_im[:sub][::-1]
    z_re = rev_re[..., None] * bb_re[None] - rev_im[..., None] * bb_im[None]
    z_im = rev_re[..., None] * bb_im[None] + rev_im[..., None] * bb_re[None]
    zz = jnp.stack([z_re, z_im]).transpose(0, 1, 2, 4, 3).reshape(2, sub, nt, LANES, p)
    wz = block_diag(zz, hg, p).astype(BF16)
    wz = wz.transpose(2, 1, 3, 0, 4).reshape(nt, wide, sw)

    cb_re = c_re[None] * pw_re[1:, :, None, :] - c_im[None] * pw_im[1:, :, None, :]
    cb_im = c_re[None] * pw_im[1:, :, None, :] + c_im[None] * pw_re[1:, :, None, :]
    yy = jnp.stack([cb_re, -cb_im]).transpose(0, 1, 2, 4, 3).reshape(2, sub, nt, gt * p, hg)
    wy = block_diag(yy, p, hg).astype(BF16)
    wy = wy.transpose(2, 0, 3, 1, 4).reshape(nt, sw, wide)

    lr, li = pw_re[sub].reshape(nt, gt * p), pw_im[sub].reshape(nt, gt * p)
    apow = jnp.stack([jnp.concatenate([lr, lr], axis=1), jnp.concatenate([-li, li], axis=1)], axis=1)
    apow = jnp.pad(apow, ((0, 0), (0, SUBLANES - 2), (0, 0)))
    dd = d_skip.reshape(nt, 1, LANES)
    return wm, wz, wy, apow, dd


def _s5_ssm(u, seq, lam_re, lam_im, log_dt, b_re, b_im, c_re, c_im, d_skip):
    n, width = u.shape
    g, p = lam_re.shape
    hg = width // g
    sub = SSM_SUB
    assert LANES % hg == 0 and width % LANES == 0
    gt = LANES // hg
    nt = width // LANES
    rb = _tile(seq // sub, SSM_ROWS)
    n_rb = seq // (rb * sub)
    n_levels = max(1, (rb - 1).bit_length())
    wm, wz, wy, apow, dd = _ssm_tables(lam_re, lam_im, log_dt, b_re, b_im, c_re, c_im, d_skip)
    wide = sub * LANES
    sw = gt * 2 * p
    return pl.pallas_call(
        functools.partial(_ssm_kernel, sub=sub, n_levels=n_levels),
        out_shape=jax.ShapeDtypeStruct((n, width), F32),
        grid=(nt, n // seq, n_rb),
        in_specs=[pl.BlockSpec((rb * sub, LANES), lambda t, b, k: (b * n_rb + k, t)),
                  pl.BlockSpec((1, wide, wide), lambda t, b, k: (t, 0, 0)),
                  pl.BlockSpec((1, wide, sw), lambda t, b, k: (t, 0, 0)),
                  pl.BlockSpec((1, sw, wide), lambda t, b, k: (t, 0, 0)),
                  pl.BlockSpec((1, apow.shape[1], sw), lambda t, b, k: (t, 0, 0)),
                  pl.BlockSpec((1, 1, LANES), lambda t, b, k: (t, 0, 0))],
        out_specs=pl.BlockSpec((rb * sub, LANES), lambda t, b, k: (b * n_rb + k, t)),
        scratch_shapes=[pltpu.VMEM((rb, wide), BF16), pltpu.VMEM((1, sw), F32)],
        compiler_params=_params(("arbitrary", "arbitrary", "arbitrary")),
        name="s5_ssm",
    )(u, wm, wz, wy, apow, dd)


def _router_kernel(x_ref, g_ref, mod_ref, wr_ref, rb_ref,
                   hp_ref, idx_ref, wts_ref, rank_ref, cnt_ref, carry_ref, *, top_k):
    i = pl.program_id(0)

    @pl.when(i == 0)
    def _():
        carry_ref[...] = jnp.zeros_like(carry_ref)

    h = _norm_mod(x_ref[...], g_ref[...], mod_ref, 3, 4)
    hp_ref[...] = _pack_halves(h)
    h_hi = h.astype(BF16)
    h_lo = (h - h_hi.astype(F32)).astype(BF16)
    nt = (((1,), (1,)), ((), ()))
    logits = (lax.dot_general(wr_ref[0], h_hi, nt, preferred_element_type=F32)
              + lax.dot_general(wr_ref[0], h_lo, nt, preferred_element_type=F32)
              + lax.dot_general(wr_ref[1], h_hi, nt, preferred_element_type=F32))
    scores = jax.nn.sigmoid(logits)
    sel = scores + rb_ref[...]
    n_e, tm = sel.shape
    e_iota = lax.broadcasted_iota(jnp.int32, (n_e, tm), 0)
    hits, vals = [], []
    for _ in range(top_k):
        mx = jnp.max(sel, axis=0, keepdims=True)
        ik = jnp.min(jnp.where(sel == mx, e_iota, n_e), axis=0, keepdims=True)
        hit = e_iota == ik
        vals.append(jnp.sum(jnp.where(hit, scores, 0.0), axis=0, keepdims=True))
        sel = jnp.where(hit, -jnp.inf, sel)
        hits.append(hit)
        idx_ref[pl.ds(len(hits) - 1, 1), :] = ik
    tot = vals[0]
    for v in vals[1:]:
        tot = tot + v
    onehot = jnp.zeros((n_e, tm), F32)
    for k in range(top_k):
        wts_ref[pl.ds(k, 1), :] = vals[k] / tot * ROUTED_SCALE
        onehot = onehot + jnp.where(hits[k], 1.0, 0.0)
    r = lax.broadcasted_iota(jnp.int32, (tm, tm), 0)
    c = lax.broadcasted_iota(jnp.int32, (tm, tm), 1)
    upper = jnp.where(r < c, 1.0, 0.0).astype(BF16)
    base = jnp.dot(onehot.astype(BF16), upper, preferred_element_type=F32) + carry_ref[...]
    for k in range(top_k):
        rk = jnp.sum(jnp.where(hits[k], base, 0.0), axis=0, keepdims=True)
        rank_ref[pl.ds(k, 1), :] = rk.astype(jnp.int32)
    carry_ref[...] = carry_ref[...] + jnp.sum(onehot, axis=1, keepdims=True)
    cnt_ref[...] = carry_ref[...]


def _router(x2, g, mod3, seq, w_router, router_bias):
    n, d = x2.shape
    n_e = w_router.shape[1]
    tm = _tile(seq, 256)
    per_b = seq // tm
    wr_hi = w_router.T.astype(BF16)
    wr_lo = (w_router.T - wr_hi.astype(F32)).astype(BF16)
    wr_t = jnp.stack([wr_hi, wr_lo])
    return pl.pallas_call(
        functools.partial(_router_kernel, top_k=TOP_K),
        out_shape=[jax.ShapeDtypeStruct((n, d // 2), U32),
                   jax.ShapeDtypeStruct((TOP_K, n), jnp.int32),
                   jax.ShapeDtypeStruct((TOP_K, n), F32),
                   jax.ShapeDtypeStruct((TOP_K, n), jnp.int32),
                   jax.ShapeDtypeStruct((n_e, 1), F32)],
        grid=(n // tm,),
        in_specs=[pl.BlockSpec((tm, d), lambda i: (i, 0)),
                  pl.BlockSpec((1, d), lambda i: (0, 0)),
                  pl.BlockSpec((1, N_MOD, d), lambda i: (i // per_b, 0, 0)),
                  pl.BlockSpec((2, n_e, d), lambda i: (0, 0, 0)),
                  pl.BlockSpec((n_e, 1), lambda i: (0, 0))],
        out_specs=[pl.BlockSpec((tm, d // 2), lambda i: (i, 0)),
                   pl.BlockSpec((TOP_K, tm), lambda i: (0, i)),
                   pl.BlockSpec((TOP_K, tm), lambda i: (0, i)),
                   pl.BlockSpec((TOP_K, tm), lambda i: (0, i)),
                   pl.BlockSpec((n_e, 1), lambda i: (0, 0))],
        scratch_shapes=[pltpu.VMEM((n_e, 1), F32)],
        compiler_params=_params(("arbitrary",)),
        name="moe_router",
    )(x2, g.reshape(1, d), mod3, wr_t, router_bias.reshape(n_e, 1))


def _dispatch_kernel(cnt_ref, pstart_ref, nu_ref, dest_ref, h_ref, xs_ref, zero_ref, sem, zsem,
                     *, top_k, n_e, rows):
    i = pl.program_id(0)
    tm = h_ref.shape[0]
    n_blocks = xs_ref.shape[0] // rows

    def issue(r, carry):
        for k in range(top_k):
            pltpu.make_async_copy(h_ref.at[pl.ds(r, 1), :],
                                  xs_ref.at[pl.ds(dest_ref[0, k, r], 1), :], sem).start(priority=k % 2)
        return carry

    lax.fori_loop(0, tm, issue, 0)

    @pl.when(i == 0)
    def _():
        zero_ref[...] = jnp.zeros_like(zero_ref)

        def zero_fill(wait):
            def go(copy):
                if wait:
                    copy.wait()
                else:
                    copy.start()

            def per_expert(e, carry):
                cnt = cnt_ref[e]
                pad = (rows - cnt % rows) % rows
                first = pstart_ref[e] + cnt

                def one_row(r, c):
                    go(pltpu.make_async_copy(zero_ref.at[pl.ds(0, 1), :],
                                             xs_ref.at[pl.ds(first + r, 1), :], zsem))
                    return c

                lax.fori_loop(0, pad % SUBLANES, one_row, 0)
                end = first + pad
                size = rows // 2
                while size >= SUBLANES:
                    end = end - (pad & size)

                    @pl.when((pad & size) != 0)
                    def _(size=size, end=end):
                        go(pltpu.make_async_copy(zero_ref.at[pl.ds(0, size), :],
                                                 xs_ref.at[pl.ds(pl.multiple_of(end, SUBLANES), size), :],
                                                 zsem))
                    size //= 2
                return carry

            lax.fori_loop(0, n_e, per_expert, 0)

            def per_block(b, carry):
                go(pltpu.make_async_copy(zero_ref, xs_ref.at[pl.ds(pl.multiple_of(b * rows, rows), rows), :],
                                         zsem))
                return carry

            lax.fori_loop(nu_ref[0], n_blocks, per_block, 0)

        zero_fill(False)
        zero_fill(True)

    for k in range(top_k):
        pltpu.make_async_copy(h_ref, xs_ref.at[pl.ds(0, tm), :], sem).wait()


def _dispatch(counts, pstart, n_used, dest3, hp, n_slots):
    n, w = hp.shape
    n_tiles, top_k, tm = dest3.shape
    n_e = counts.shape[0]
    return pl.pallas_call(
        functools.partial(_dispatch_kernel, top_k=top_k, n_e=n_e, rows=MOE_ROWS),
        out_shape=jax.ShapeDtypeStruct((n_slots, w), U32),
        grid_spec=pltpu.PrefetchScalarGridSpec(
            num_scalar_prefetch=3,
            grid=(n_tiles,),
            in_specs=[pl.BlockSpec((1, top_k, tm), lambda i, c, p, u: (i, 0, 0),
                                   memory_space=pltpu.SMEM),
                      pl.BlockSpec((tm, w), lambda i, c, p, u: (i, 0))],
            out_specs=pl.BlockSpec(memory_space=pl.ANY),
            scratch_shapes=[pltpu.VMEM((MOE_ROWS, w), U32), pltpu.SemaphoreType.DMA(()),
                            pltpu.SemaphoreType.DMA(())],
        ),
        compiler_params=_params(("arbitrary",)),
        name="moe_dispatch",
    )(counts, pstart, n_used, dest3, hp)


def _swiglu_packed(xp, wg, wu, wd):
    xh, xl = _unpack_halves(xp)
    half = xh.shape[1]
    gate = (jnp.dot(xh, wg[:half], preferred_element_type=F32)
            + jnp.dot(xl, wg[half:], preferred_element_type=F32))
    up = (jnp.dot(xh, wu[:half], preferred_element_type=F32)
          + jnp.dot(xl, wu[half:], preferred_element_type=F32))
    hm = (gate * jax.nn.sigmoid(gate) * up).astype(BF16)
    return jnp.dot(hm, wd, preferred_element_type=F32)


def _expert_kernel(be_ref, nu_ref, x_ref, wg_ref, wu_ref, wd_ref, o_ref):
    b = pl.program_id(0)

    @pl.when(b < nu_ref[0])
    def _():
        y = _swiglu_packed(x_ref[...], wg_ref[0], wu_ref[0], wd_ref[0])
        o_ref[...] = _pack_halves(y)

    @pl.when(b >= nu_ref[0])
    def _():
        o_ref[...] = jnp.zeros_like(o_ref)


def _experts(block_e, n_used, xs, wg, wu, wd):
    n_slots, w = xs.shape
    n_e, d, de = wg.shape
    rows = MOE_ROWS
    n_blocks = n_slots // rows
    return pl.pallas_call(
        _expert_kernel,
        out_shape=jax.ShapeDtypeStruct((n_slots, w), U32),
        grid_spec=pltpu.PrefetchScalarGridSpec(
            num_scalar_prefetch=2,
            grid=(n_blocks,),
            in_specs=[pl.BlockSpec((rows, w), lambda b, be, nu: (jnp.minimum(b, nu[0] - 1), 0)),
                      pl.BlockSpec((1, d, de), lambda b, be, nu: (be[b], 0, 0)),
                      pl.BlockSpec((1, d, de), lambda b, be, nu: (be[b], 0, 0)),
                      pl.BlockSpec((1, de, d), lambda b, be, nu: (be[b], 0, 0))],
            out_specs=pl.BlockSpec((rows, w), lambda b, be, nu: (b, 0)),
        ),
        compiler_params=_params(("arbitrary",)),
        name="moe_experts",
    )(block_e, n_used, xs, wg, wu, wd)


def _shared_kernel(x_ref, wg_ref, wu_ref, wd_ref, o_ref):
    o_ref[...] = _swiglu_packed(x_ref[...], wg_ref[...], wu_ref[...], wd_ref[...])


def _shared_expert(hp, wg, wu, wd):
    n, w = hp.shape
    d, ds = wg.shape
    tm = _tile(n, 512)
    return pl.pallas_call(
        _shared_kernel,
        out_shape=jax.ShapeDtypeStruct((n, d), F32),
        grid=(n // tm,),
        in_specs=[pl.BlockSpec((tm, w), lambda i: (i, 0)),
                  pl.BlockSpec((d, ds), lambda i: (0, 0)),
                  pl.BlockSpec((d, ds), lambda i: (0, 0)),
                  pl.BlockSpec((ds, d), lambda i: (0, 0))],
        out_specs=pl.BlockSpec((tm, d), lambda i: (i, 0)),
        compiler_params=_params(("arbitrary",)),
        name="moe_shared",
    )(hp, wg, wu, wd)


def _combine_kernel(dest_ref, next_ref, w_ref, x_ref, sh_ref, mod_ref, g_ref, ys_ref, o_ref, buf_ref, sems,
                    *, top_k):
    tm, d = x_ref.shape
    half = d // 2
    i = pl.program_id(0)
    slot = i % 2

    def gather(idx_ref, slot):
        def issue(r, carry):
            for k in range(top_k):
                pltpu.make_async_copy(ys_ref.at[pl.ds(idx_ref[0, k, r], 1), :],
                                      buf_ref.at[slot, k, pl.ds(r, 1), :],
                                      sems.at[slot]).start(priority=k % 2)
            return carry
        lax.fori_loop(0, tm, issue, 0)

    @pl.when(i == 0)
    def _():
        gather(dest_ref, 0)

    @pl.when(i + 1 < pl.num_programs(0))
    def _():
        gather(next_ref, 1 - slot)

    for k in range(top_k):
        pltpu.make_async_copy(ys_ref.at[pl.ds(0, tm), :], buf_ref.at[slot, k], sems.at[slot]).wait()

    acc_h = jnp.zeros((tm, half), F32)
    acc_l = jnp.zeros((tm, half), F32)
    for k in range(top_k):
        p = buf_ref[slot, k]
        wk = w_ref[:, k:k + 1]
        acc_h = acc_h + wk * pltpu.bitcast(p & jnp.uint32(HI_MASK), F32)
        acc_l = acc_l + wk * pltpu.bitcast(p << 16, F32)
    gate = mod_ref[0, 5:6, :]
    x_h = x_ref[:, :half] + gate[:, :half] * (acc_h + sh_ref[:, :half])
    x_l = x_ref[:, half:] + gate[:, half:] * (acc_l + sh_ref[:, half:])
    ms = (jnp.sum(x_h * x_h, axis=1, keepdims=True) + jnp.sum(x_l * x_l, axis=1, keepdims=True)) / d
    inv = lax.rsqrt(ms + NORM_EPS)
    o_ref[:, :half] = x_h * inv * g_ref[:, :half]
    o_ref[:, half:] = x_l * inv * g_ref[:, half:]


def _combine(dest3, wts_t, x2, shared, mod3, g_final, ys, seq):
    n, d = x2.shape
    n_tiles, top_k, tm = dest3.shape
    per_b = seq // tm
    return pl.pallas_call(
        functools.partial(_combine_kernel, top_k=top_k),
        out_shape=jax.ShapeDtypeStruct((n, d), F32),
        grid=(n_tiles,),
        in_specs=[pl.BlockSpec((1, top_k, tm), lambda i: (i, 0, 0), memory_space=pltpu.SMEM),
                  pl.BlockSpec((1, top_k, tm), lambda i: (jnp.minimum(i + 1, n_tiles - 1), 0, 0),
                               memory_space=pltpu.SMEM),
                  pl.BlockSpec((tm, top_k), lambda i: (i, 0)),
                  pl.BlockSpec((tm, d), lambda i: (i, 0)),
                  pl.BlockSpec((tm, d), lambda i: (i, 0)),
                  pl.BlockSpec((1, N_MOD, d), lambda i: (i // per_b, 0, 0)),
                  pl.BlockSpec((1, d), lambda i: (0, 0)),
                  pl.BlockSpec(memory_space=pl.ANY)],
        out_specs=pl.BlockSpec((tm, d), lambda i: (i, 0)),
        scratch_shapes=[pltpu.VMEM((2, top_k, tm, d // 2), U32), pltpu.SemaphoreType.DMA((2,))],
        compiler_params=_params(("arbitrary",)),
        name="moe_combine",
    )(dest3, dest3, wts_t, x2, shared, mod3, g_final.reshape(1, d), ys)


def _retile(a, tm):
    k, n = a.shape
    return a.reshape(k, n // tm, tm).transpose(1, 0, 2)


def kernel(x, c, w_ada, b_ada, g_mix, w_in, b_forget, lam_re, lam_im, log_dt, b_re, b_im, c_re, c_im, d_skip, w_glu, w_proj_attn, w_proj_ssm, w_out, g_ffn, w_router, router_bias, w_gate_e, w_up_e, w_down_e, w_gate_s, w_up_s, w_down_s, g_final):
    bsz, seq, d = x.shape
    n = bsz * seq
    depth = w_ada.shape[0]
    n_heads = b_forget.shape[1]
    attn_w = w_proj_attn.shape[1]
    dh = attn_w // n_heads
    ssm_w = w_proj_ssm.shape[1]
    n_e = w_router.shape[2]

    assert depth == 1, "the final rms_norm is fused into the last layer's combine kernel"
    c_pad = jnp.zeros((8, d), F32).at[:bsz].set(c)
    x2 = x.reshape(n, d)
    out = None
    for l in range(depth):
        mod = _ada(c_pad, w_ada[l], b_ada[l])
        mod3 = mod[:bsz].reshape(bsz, N_MOD, d)

        h = _norm_modulate(x2, g_mix[l], mod3, seq, 0, 1)
        w = w_in[l]
        o_f = 3 * attn_w
        o_u = o_f + n_heads
        o_ga = o_u + ssm_w
        w_qkv = w[:, :o_f].astype(BF16)
        q_scale = jnp.concatenate([jnp.full((attn_w,), dh ** -0.5 * LOG2E, F32),
                                   jnp.ones((2 * attn_w,), F32)]).reshape(1, o_f)
        qkv = _mm([h], [(0, w_qkv, 0)], [_row_extra(q_scale)],
                  lambda accs, ex: accs[0] * ex[0][...], o_f, BF16, "proj_qkv")
        w_f = jnp.zeros((d, LANES), F32).at[:, :n_heads].set(w[:, o_f:o_u]).astype(BF16)
        f_logit = _mm([h], [(0, w_f, 0)], [], lambda accs, ex: accs[0], LANES, F32, "proj_forget")
        u = _mm([h], [(0, w[:, o_u:o_ga].astype(BF16), 0)], [], lambda accs, ex: accs[0],
                ssm_w, F32, "proj_u")
        gates = _mm([h], [(0, w[:, o_ga:].astype(BF16), 0)], [],
                    lambda accs, ex: jax.nn.sigmoid(accs[0]), 2 * d, BF16, "proj_gates")

        b_row = jnp.zeros((1, LANES), F32).at[0, :n_heads].set(b_forget[l])
        cum = _forget_cumsum(f_logit.reshape(bsz, seq, LANES), b_row)[:, :, :n_heads]
        attn, (wg_e, wu_e, wd_e) = _attention(qkv.reshape(bsz, seq, o_f), cum, n_heads, dh,
                                              side=(w_gate_e[l], w_up_e[l], w_down_e[l]))
        attn = attn.reshape(n, attn_w)

        sg = _s5_ssm(u, seq, lam_re[l], lam_im[l], log_dt[l], b_re[l], b_im[l],
                     c_re[l], c_im[l], d_skip[l])
        w_g = w_glu[l].astype(BF16)
        ssm = _mm([sg], [(0, w_g, 0), (0, w_g, ssm_w)], [],
                  lambda accs, ex: accs[0] * jax.nn.sigmoid(accs[1]), ssm_w, BF16, "ssm_glu")

        merged = _mm([attn, ssm],
                     [(0, w_proj_attn[l].astype(BF16), 0), (1, w_proj_ssm[l].astype(BF16), 0)],
                     [_tile_extra(gates), _tile_extra(gates, d)],
                     lambda accs, ex: (ex[0][...].astype(F32) * accs[0]
                                       + ex[1][...].astype(F32) * accs[1]),
                     d, BF16, "merge_proj")
        x2 = _mm([merged], [(0, w_out[l].astype(BF16), 0)], [_tile_extra(x2), _mod_extra(mod3, seq)],
                 lambda accs, ex: ex[0][...] + ex[1][0, 2:3, :] * accs[0], d, F32, "out_proj",
                 row_unit=seq)

        hp, idx_t, wts_t, rank_t, cnt = _router(x2, g_ffn[l], mod3, seq, w_router[l], router_bias[l])
        counts = cnt[:, 0].astype(jnp.int32)
        padded = (counts + MOE_ROWS - 1) // MOE_ROWS * MOE_ROWS
        pend = jnp.cumsum(padded)
        pstart = pend - padded
        e_ids = jnp.arange(n_e, dtype=jnp.int32)
        first = jnp.sum(jnp.where(idx_t[None] == e_ids[:, None, None], pstart[:, None, None], 0), axis=0)
        dest = first + rank_t
        n_blocks = -(-(n * TOP_K + n_e * (MOE_ROWS - 1)) // MOE_ROWS)
        n_used = (pend[-1] // MOE_ROWS).astype(jnp.int32).reshape(1)
        blk = jnp.minimum(jnp.arange(n_blocks, dtype=jnp.int32), n_used - 1)
        block_e = jnp.minimum(jnp.sum((blk * MOE_ROWS)[:, None] >= pend[None, :], axis=1),
                              n_e - 1).astype(jnp.int32)
        tm_d = _tile(seq, 256)
        xs = _dispatch(counts, pstart.astype(jnp.int32), n_used, _retile(dest, tm_d), hp,
                       n_blocks * MOE_ROWS)
        ys = _experts(block_e, n_used, xs, wg_e, wu_e, wd_e)
        shared = _shared_expert(hp, w_gate_s[l].astype(BF16), w_up_s[l].astype(BF16),
                                w_down_s[l].astype(BF16))
        tm_c = _tile(seq, 128)
        out = _combine(_retile(dest, tm_c), wts_t.T, x2, shared, mod3, g_final, ys, seq)
    return out.reshape(bsz, seq, d)
```

```python
import functools
import math

import jax
import jax.numpy as jnp
from jax import lax
from jax.experimental import pallas as pl
from jax.experimental.pallas import tpu as pltpu

TOP_K = 8
ROUTED_SCALE = 2.5
NORM_EPS = 1e-6
N_MOD = 6
SSM_SUB = 8
SSM_ROWS = 256
MOE_ROWS = 256
ATTN_SUB_ROWS = 512
LOG2E = 1.4426950408889634

LANES = 128
SUBLANES = 8
VMEM_LIMIT = 56 * 1024 * 1024

F32 = jnp.float32
BF16 = jnp.bfloat16
U32 = jnp.uint32
HI_MASK = 0xFFFF0000


def _tile(n, pref):
    if n <= pref:
        return n
    t = pref
    while n % t:
        t //= 2
    return t


def _params(sem):
    return pltpu.CompilerParams(dimension_semantics=sem, vmem_limit_bytes=VMEM_LIMIT)


def _pack_halves(y):
    w = y.shape[1] // 2
    bits = pltpu.bitcast(y.astype(BF16).astype(F32), U32)
    return bits[:, :w] | (bits[:, w:] >> 16)


def _unpack_halves(p):
    hi = pltpu.bitcast(p & jnp.uint32(HI_MASK), F32)
    lo = pltpu.bitcast(p << 16, F32)
    return hi.astype(BF16), lo.astype(BF16)


def _ada_kernel(c_ref, w_ref, b_ref, o_ref):
    c = c_ref[...]
    cond = c * jax.nn.sigmoid(c)
    o_ref[...] = jnp.dot(cond.astype(BF16), w_ref[...].astype(BF16),
                         preferred_element_type=F32) + b_ref[...]


def _ada(c_pad, w_ada, b_ada):
    rows, d = c_pad.shape
    n = w_ada.shape[1]
    tn = _tile(n, 512)
    return pl.pallas_call(
        _ada_kernel,
        out_shape=jax.ShapeDtypeStruct((rows, n), F32),
        grid=(n // tn,),
        in_specs=[pl.BlockSpec((rows, d), lambda j: (0, 0)),
                  pl.BlockSpec((d, tn), lambda j: (0, j)),
                  pl.BlockSpec((1, tn), lambda j: (0, j))],
        out_specs=pl.BlockSpec((rows, tn), lambda j: (0, j)),
        compiler_params=_params(("arbitrary",)),
        name="ada_mod",
    )(c_pad, w_ada, b_ada.reshape(1, n))


def _norm_mod(x, g_row, mod_ref, shift_row, scale_row):
    ms = jnp.mean(x * x, axis=-1, keepdims=True)
    y = x * lax.rsqrt(ms + NORM_EPS) * g_row
    sc = mod_ref[0, scale_row:scale_row + 1, :]
    sh = mod_ref[0, shift_row:shift_row + 1, :]
    return y * (1.0 + sc) + sh


def _norm_mod_kernel(x_ref, g_ref, mod_ref, o_ref, *, shift_row, scale_row):
    o_ref[...] = _norm_mod(x_ref[...], g_ref[...], mod_ref, shift_row, scale_row).astype(o_ref.dtype)


def _norm_modulate(x2, g, mod3, seq, shift_row, scale_row):
    n, d = x2.shape
    tm = _tile(seq, 256)
    per_b = seq // tm
    return pl.pallas_call(
        functools.partial(_norm_mod_kernel, shift_row=shift_row, scale_row=scale_row),
        out_shape=jax.ShapeDtypeStruct((n, d), BF16),
        grid=(n // tm,),
        in_specs=[pl.BlockSpec((tm, d), lambda i: (i, 0)),
                  pl.BlockSpec((1, d), lambda i: (0, 0)),
                  pl.BlockSpec((1, N_MOD, d), lambda i: (i // per_b, 0, 0))],
        out_specs=pl.BlockSpec((tm, d), lambda i: (i, 0)),
        compiler_params=_params(("arbitrary",)),
        name="norm_mod",
    )(x2, g.reshape(1, d), mod3)


def _mm_kernel(*refs, n_a, a_of_b, epilogue):
    o_ref = refs[-1]
    n_b = len(a_of_b)
    lhs = [refs[a][...].astype(BF16) for a in range(n_a)]
    accs = [jnp.dot(lhs[a_of_b[p]], refs[n_a + p][...], preferred_element_type=F32)
            for p in range(n_b)]
    extras = refs[n_a + n_b:-1]
    o_ref[...] = epilogue(accs, extras).astype(o_ref.dtype)


def _mm(a_list, b_list, extras, epilogue, n_out, out_dtype, name, row_unit=None,
        tm_pref=1024, tn_pref=1024):
    m = a_list[0].shape[0]
    tm = _tile(row_unit or m, tm_pref)
    tn = _tile(n_out, tn_pref)
    in_specs, args = [], []
    for a in a_list:
        in_specs.append(pl.BlockSpec((tm, a.shape[1]), lambda i, j: (i, 0)))
        args.append(a)
    for _, b, col in b_list:
        in_specs.append(pl.BlockSpec((b.shape[0], tn), lambda i, j, off=col // tn: (0, j + off)))
        args.append(b)
    for arr, bshape, imap in extras:
        in_specs.append(pl.BlockSpec(bshape(tm, tn), imap(tm, tn)))
        args.append(arr)
    return pl.pallas_call(
        functools.partial(_mm_kernel, n_a=len(a_list), a_of_b=tuple(ai for ai, _, _ in b_list),
                          epilogue=epilogue),
        out_shape=jax.ShapeDtypeStruct((m, n_out), out_dtype),
        grid=(m // tm, n_out // tn),
        in_specs=in_specs,
        out_specs=pl.BlockSpec((tm, tn), lambda i, j: (i, j)),
        compiler_params=_params(("arbitrary", "arbitrary")),
        name=name,
    )(*args)


def _row_extra(row):
    return (row, lambda tm, tn: (1, tn), lambda tm, tn: (lambda i, j: (0, j)))


def _tile_extra(arr, col=0):
    return (arr, lambda tm, tn: (tm, tn), lambda tm, tn: (lambda i, j: (i, j + col // tn)))


def _mod_extra(mod3, seq):
    return (mod3, lambda tm, tn: (1, N_MOD, tn),
            lambda tm, tn: (lambda i, j: (i // (seq // tm), 0, j)))


def _cum_kernel(f_ref, b_ref, o_ref, *, chunk):
    seq = f_ref.shape[1]
    x = f_ref[0] + b_ref[...]
    lf = jnp.minimum(x, 0.0) - jnp.log(1.0 + jnp.exp(-jnp.abs(x)))
    o_ref[0] = lf
    r = lax.broadcasted_iota(jnp.int32, (chunk, chunk), 0)
    c = lax.broadcasted_iota(jnp.int32, (chunk, chunk), 1)
    tri = jnp.where(c <= r, 1.0, 0.0).astype(F32)

    def body(i, carry):
        start = pl.multiple_of(i * chunk, chunk)
        blk = o_ref[0, pl.ds(start, chunk), :]
        cs = jnp.dot(tri, blk, precision=lax.Precision.HIGHEST,
                     preferred_element_type=F32) + carry
        o_ref[0, pl.ds(start, chunk), :] = cs
        return cs[chunk - 1:chunk, :]

    lax.fori_loop(0, seq // chunk, body, jnp.zeros((1, f_ref.shape[2]), F32))


def _forget_cumsum(f3, b_row):
    bsz, seq, w = f3.shape
    chunk = _tile(seq, 128)
    return pl.pallas_call(
        functools.partial(_cum_kernel, chunk=chunk),
        out_shape=jax.ShapeDtypeStruct((bsz, seq, w), F32),
        grid=(bsz,),
        in_specs=[pl.BlockSpec((1, seq, w), lambda b: (b, 0, 0)),
                  pl.BlockSpec((1, w), lambda b: (0, 0))],
        out_specs=pl.BlockSpec((1, seq, w), lambda b: (b, 0, 0)),
        compiler_params=_params(("arbitrary",)),
        name="forget_cumsum",
    )(f3, b_row)


def _attn_kernel(c0_ref, *refs, blk, n_sub, n_heads, n_q, n_side, paired):
    n_qb = 2 if paired else 1
    q_refs = refs[:n_qb]
    k_ref, v_ref, ck_ref = refs[n_qb:n_qb + 3]
    side_in = refs[n_qb + 3:n_qb + 3 + n_side]
    o_refs = refs[n_qb + 3 + n_side:2 * n_qb + 3 + n_side]
    side_out = refs[2 * n_qb + 3 + n_side:2 * n_qb + 3 + 2 * n_side]
    sa_ref, sb_ref = refs[2 * n_qb + 3 + 2 * n_side:]
    b = pl.program_id(0)
    h = pl.program_id(1)
    step = pl.program_id(2)
    for w_ref, wo_ref in zip(side_in, side_out):
        wo_ref[...] = w_ref[...].astype(wo_ref.dtype)
    blocks = [step, n_q - 1 - step] if paired else [step]
    for i, q_ref, o_ref in zip(blocks, q_refs, o_refs):
        _attn_sweep(i, c0_ref[(b * n_heads + h) * n_q + i], q_ref, k_ref, v_ref, ck_ref, o_ref,
                    sa_ref, sb_ref, blk=blk, n_sub=n_sub)


def _attn_sweep(i, c0, q_ref, k_ref, v_ref, ck_ref, o_ref, sa_ref, sb_ref, *, blk, n_sub):
    sub = blk // n_sub
    qs = [q_ref[0, a * sub:(a + 1) * sub, :] for a in range(n_sub)]

    def logits(j, s_ref):
        kb = k_ref[0, pl.ds(pl.multiple_of(j * blk, blk), blk), :]
        bias = (c0 - ck_ref[0, 0, j]) * LOG2E
        for a, q in enumerate(qs):
            s_ref[a * sub:(a + 1) * sub, :] = lax.dot_general(
                q, kb, (((1,), (1,)), ((), ())), preferred_element_type=F32) + bias

    def update(j, s_ref, state, diagonal=False):
        vb = v_ref[0, pl.ds(pl.multiple_of(j * blk, blk), blk), :]
        out = []
        for a, (m, l, acc) in enumerate(state):
            t = s_ref[a * sub:(a + 1) * sub, :]
            if diagonal:
                row = lax.broadcasted_iota(jnp.int32, (sub, blk), 0)
                col = lax.broadcasted_iota(jnp.int32, (sub, blk), 1)
                t = jnp.where(col <= row + a * sub, t, -jnp.inf)
            m_new = jnp.maximum(m, jnp.max(t, axis=1, keepdims=True))
            alpha = jnp.exp2(m - m_new)
            p = jnp.exp2(t - m_new)
            l = alpha * l + jnp.sum(p, axis=1, keepdims=True)
            acc = alpha * acc + jnp.dot(p.astype(BF16), vb, preferred_element_type=F32)
            out.append((m_new, l, acc))
        return tuple(out)

    def body(pair, state):
        j = 2 * pair
        logits(j + 1, sb_ref)
        state = update(j, sa_ref, state)
        logits(j + 2, sa_ref)
        return update(j + 1, sb_ref, state)

    def body2(quad, state):
        return body(2 * quad + 1, body(2 * quad, state))

    init = tuple((jnp.full((sub, 1), -jnp.inf, F32), jnp.zeros((sub, 1), F32),
                  jnp.zeros((sub, q_ref.shape[2]), F32)) for _ in range(n_sub))
    logits(0, sa_ref)
    state = lax.fori_loop(0, i // 4, body2, init)
    state = lax.fori_loop(2 * (i // 4), i // 2, body, state)

    def finish(state):
        for a, (m, l, acc) in enumerate(state):
            o_ref[0, a * sub:(a + 1) * sub, :] = (acc / l).astype(o_ref.dtype)

    @pl.when(i % 2 == 0)
    def _():
        finish(update(i, sa_ref, state, diagonal=True))

    @pl.when(i % 2 == 1)
    def _():
        logits(i, sb_ref)
        finish(update(i, sb_ref, update(i - 1, sa_ref, state), diagonal=True))


def _attention(qkv3, cum, n_heads, dh, side=()):
    bsz, seq, _ = qkv3.shape
    blk = _tile(seq, 512)
    n_q = seq // blk
    paired = n_q % 2 == 0
    n_qs = n_q // 2 if paired else n_q
    n_steps = bsz * n_heads * n_qs
    cum_t = cum.transpose(0, 2, 1)
    ck = cum_t.reshape(bsz, n_heads, n_q, 1, blk)
    c0 = cum_t[:, :, ::blk].reshape(-1)
    side2 = [w.reshape(-1, w.shape[-1]) for w in side]
    bf16_rows = 2 * SUBLANES
    assert all(w.shape[0] % (n_steps * bf16_rows) == 0 for w in side2)

    def slab(w):
        return pl.BlockSpec((w.shape[0] // n_steps, w.shape[1]),
                            lambda b, h, i, c: ((b * n_heads + h) * n_qs + i, 0))

    q_specs = [pl.BlockSpec((1, blk, dh), lambda b, h, i, c: (b, i, h))]
    o_specs = [pl.BlockSpec((1, blk, dh), lambda b, h, i, c: (b, i, h))]
    o_rows = [seq]
    if paired:
        q_specs.append(pl.BlockSpec((1, blk, dh), lambda b, h, i, c: (b, n_q - 1 - i, h)))
        o_specs.append(pl.BlockSpec((1, blk, dh), lambda b, h, i, c: (b, n_qs - 1 - i, h)))
        o_rows = [seq // 2, seq // 2]
    n_o = len(o_specs)
    out = pl.pallas_call(
        functools.partial(_attn_kernel, blk=blk, n_sub=max(1, blk // ATTN_SUB_ROWS),
                          n_heads=n_heads, n_q=n_q, n_side=len(side2), paired=paired),
        out_shape=[jax.ShapeDtypeStruct((bsz, r, n_heads * dh), BF16) for r in o_rows]
        + [jax.ShapeDtypeStruct(w.shape, BF16) for w in side2],
        grid_spec=pltpu.PrefetchScalarGridSpec(
            num_scalar_prefetch=1,
            grid=(bsz, n_heads, n_qs),
            in_specs=q_specs
            + [pl.BlockSpec((1, seq, dh), lambda b, h, i, c: (b, 0, n_heads + h)),
               pl.BlockSpec((1, seq, dh), lambda b, h, i, c: (b, 0, 2 * n_heads + h)),
               pl.BlockSpec((1, 1, n_q, 1, blk), lambda b, h, i, c: (b, h, 0, 0, 0))]
            + [slab(w) for w in side2],
            out_specs=o_specs + [slab(w) for w in side2],
            scratch_shapes=[pltpu.VMEM((blk, blk), F32), pltpu.VMEM((blk, blk), F32)],
        ),
        compiler_params=_params(("arbitrary", "arbitrary", "arbitrary")),
        name="fox_attention",
    )(c0, *([qkv3] * n_o), qkv3, qkv3, ck, *side2)
    attn = out[0] if n_o == 1 else jnp.concatenate(out[:n_o], axis=1)
    return attn, [o.reshape(w.shape) for o, w in zip(out[n_o:], side)]


def _gelu_tanh(x):
    return 0.5 * x * (1.0 + jnp.tanh(math.sqrt(2.0 / math.pi) * (x + 0.044715 * (x * x * x))))


def _ssm_kernel(u_ref, wm_ref, wz_ref, wy_ref, apow_ref, dd_ref, o_ref, xcat_ref, carry_ref,
                *, sub, n_levels):
    rb = u_ref.shape[0] // sub

    @pl.when(pl.program_id(2) == 0)
    def _():
        carry_ref[...] = jnp.zeros_like(carry_ref)

    for i in range(sub):
        xcat_ref[:, i * LANES:(i + 1) * LANES] = u_ref[pl.ds(i, rb, stride=sub), :].astype(BF16)
    x = xcat_ref[...]
    y = jnp.dot(x, wm_ref[0], preferred_element_type=F32)
    s = jnp.dot(x, wz_ref[0], preferred_element_type=F32)
    half = s.shape[1] // 2
    row = lax.broadcasted_iota(jnp.int32, s.shape, 0)
    carry = carry_ref[...]

    a1 = apow_ref[0, 0:1, :]
    a2 = apow_ref[0, 1:2, :]

    def times_a(v):
        return a1 * v + a2 * pltpu.roll(v, half, axis=1)

    s = s + jnp.where(row == 0, times_a(carry), 0.0)
    for lvl in range(n_levels):
        shift = 1 << lvl
        s = s + times_a(jnp.where(row >= shift, pltpu.roll(s, shift, axis=0), 0.0))
        a1, a2 = a1 * a1 - a2 * a2, 2.0 * a1 * a2
    s_in = jnp.where(row >= 1, pltpu.roll(s, 1, axis=0), carry)
    carry_ref[...] = s[rb - 1:rb, :]
    y = y + jnp.dot(s_in.astype(BF16), wy_ref[0], preferred_element_type=F32)
    for j in range(sub):
        yj = y[:, j * LANES:(j + 1) * LANES] + dd_ref[0] * u_ref[pl.ds(j, rb, stride=sub), :]
        o_ref[pl.ds(j, rb, stride=sub), :] = _gelu_tanh(yj)


def _ssm_tables(lam_re, lam_im, log_dt, b_re, b_im, c_re, c_im, d_skip):
    g, p = lam_re.shape
    hg = b_re.shape[-1]
    sub = SSM_SUB
    dt = jnp.exp(log_dt)[:, None]
    mag = jnp.exp(lam_re * dt)
    a_re = mag * jnp.cos(lam_im * dt)
    a_im = mag * jnp.sin(lam_im * dt)
    den = lam_re * lam_re + lam_im * lam_im
    k_re = ((a_re - 1) * lam_re + a_im * lam_im) / den
    k_im = (a_im * lam_re - (a_re - 1) * lam_im) / den
    bb_re = k_re[..., None] * b_re - k_im[..., None] * b_im
    bb_im = k_re[..., None] * b_im + k_im[..., None] * b_re

    tau = jnp.arange(sub + 1, dtype=F32)[:, None, None]
    pmag = jnp.exp(tau * (lam_re * dt))
    pw_re = pmag * jnp.cos(tau * (lam_im * dt))
    pw_im = pmag * jnp.sin(tau * (lam_im * dt))

    ca_re = c_re[None] * pw_re[:sub, :, None, :] - c_im[None] * pw_im[:sub, :, None, :]
    ca_im = c_re[None] * pw_im[:sub, :, None, :] + c_im[None] * pw_re[:sub, :, None, :]
    kk = (jnp.einsum('tgop,gpi->tgio', ca_re, bb_re, precision=lax.Precision.HIGHEST)
          - jnp.einsum('tgop,gpi->tgio', ca_im, bb_im, precision=lax.Precision.HIGHEST))

    gt = LANES // hg
    nt = g // gt
    wide = sub * LANES
    sw = gt * 2 * p

    def block_diag(x, rows_per_group, cols_per_group):
        n_r, n_c = gt * rows_per_group, gt * cols_per_group
        r = lax.broadcasted_iota(jnp.int32, (n_r, n_c), 0) // rows_per_group
        c = lax.broadcasted_iota(jnp.int32, (n_r, n_c), 1) // cols_per_group
        return jnp.where(r == c, jnp.tile(x, (1,) * (x.ndim - 1) + (gt,)), 0.0)

    bd = block_diag(kk.reshape(sub, nt, LANES, hg), hg, hg).astype(BF16)
    zero = jnp.zeros_like(bd[0])
    wm = jnp.stack([jnp.concatenate([zero] * i + [bd[k] for k in range(sub - i)], axis=-1)
                    for i in range(sub)], axis=1).reshape(nt, wide, wide)

    rev_re = pw_re[:sub][::-1]
    rev_im = pw_im[:sub][::-1]
    z_re = rev_re[..., None] * bb_re[None] - rev_im[..., None] * bb_im[None]
    z_im = rev_re[..., None] * bb_im[None] + rev_im[..., None] * bb_re[None]
    zz = jnp.stack([z_re, z_im]).transpose(0, 1, 2, 4, 3).reshape(2, sub, nt, LANES, p)
    wz = block_diag(zz, hg, p).astype(BF16)
    wz = wz.transpose(2, 1, 3, 0, 4).reshape(nt, wide, sw)

    cb_re = c_re[None] * pw_re[1:, :, None, :] - c_im[None] * pw_im[1:, :, None, :]
    cb_im = c_re[None] * pw_im[1:, :, None, :] + c_im[None] * pw_re[1:, :, None, :]
    yy = jnp.stack([cb_re, -cb_im]).transpose(0, 1, 2, 4, 3).reshape(2, sub, nt, gt * p, hg)
    wy = block_diag(yy, p, hg).astype(BF16)
    wy = wy.transpose(2, 0, 3, 1, 4).reshape(nt, sw, wide)

    lr, li = pw_re[sub].reshape(nt, gt * p), pw_im[sub].reshape(nt, gt * p)
    apow = jnp.stack([jnp.concatenate([lr, lr], axis=1), jnp.concatenate([-li, li], axis=1)], axis=1)
    apow = jnp.pad(apow, ((0, 0), (0, SUBLANES - 2), (0, 0)))
    dd = d_skip.reshape(nt, 1, LANES)
    return wm, wz, wy, apow, dd


def _s5_ssm(u, seq, lam_re, lam_im, log_dt, b_re, b_im, c_re, c_im, d_skip):
    n, width = u.shape
    g, p = lam_re.shape
    hg = width // g
    sub = SSM_SUB
    assert LANES % hg == 0 and width % LANES == 0
    gt = LANES // hg
    nt = width // LANES
    rb = _tile(seq // sub, SSM_ROWS)
    n_rb = seq // (rb * sub)
    n_levels = max(1, (rb - 1).bit_length())
    wm, wz, wy, apow, dd = _ssm_tables(lam_re, lam_im, log_dt, b_re, b_im, c_re, c_im, d_skip)
    wide = sub * LANES
    sw = gt * 2 * p
    return pl.pallas_call(
        functools.partial(_ssm_kernel, sub=sub, n_levels=n_levels),
        out_shape=jax.ShapeDtypeStruct((n, width), F32),
        grid=(nt, n // seq, n_rb),
        in_specs=[pl.BlockSpec((rb * sub, LANES), lambda t, b, k: (b * n_rb + k, t)),
                  pl.BlockSpec((1, wide, wide), lambda t, b, k: (t, 0, 0)),
                  pl.BlockSpec((1, wide, sw), lambda t, b, k: (t, 0, 0)),
                  pl.BlockSpec((1, sw, wide), lambda t, b, k: (t, 0, 0)),
                  pl.BlockSpec((1, apow.shape[1], sw), lambda t, b, k: (t, 0, 0)),
                  pl.BlockSpec((1, 1, LANES), lambda t, b, k: (t, 0, 0))],
        out_specs=pl.BlockSpec((rb * sub, LANES), lambda t, b, k: (b * n_rb + k, t)),
        scratch_shapes=[pltpu.VMEM((rb, wide), BF16), pltpu.VMEM((1, sw), F32)],
        compiler_params=_params(("arbitrary", "arbitrary", "arbitrary")),
        name="s5_ssm",
    )(u, wm, wz, wy, apow, dd)


def _router_kernel(x_ref, g_ref, mod_ref, wr_ref, rb_ref,
                   hp_ref, idx_ref, wts_ref, rank_ref, cnt_ref, carry_ref, *, top_k):
    i = pl.program_id(0)

    @pl.when(i == 0)
    def _():
        carry_ref[...] = jnp.zeros_like(carry_ref)

    h = _norm_mod(x_ref[...], g_ref[...], mod_ref, 3, 4)
    hp_ref[...] = _pack_halves(h)
    h_hi = h.astype(BF16)
    h_lo = (h - h_hi.astype(F32)).astype(BF16)
    nt = (((1,), (1,)), ((), ()))
    logits = (lax.dot_general(wr_ref[0], h_hi, nt, preferred_element_type=F32)
              + lax.dot_general(wr_ref[0], h_lo, nt, preferred_element_type=F32)
              + lax.dot_general(wr_ref[1], h_hi, nt, preferred_element_type=F32))
    scores = jax.nn.sigmoid(logits)
    sel = scores + rb_ref[...]
    n_e, tm = sel.shape
    e_iota = lax.broadcasted_iota(jnp.int32, (n_e, tm), 0)
    hits, vals = [], []
    for _ in range(top_k):
        mx = jnp.max(sel, axis=0, keepdims=True)
        ik = jnp.min(jnp.where(sel == mx, e_iota, n_e), axis=0, keepdims=True)
        hit = e_iota == ik
        vals.append(jnp.sum(jnp.where(hit, scores, 0.0), axis=0, keepdims=True))
        sel = jnp.where(hit, -jnp.inf, sel)
        hits.append(hit)
        idx_ref[pl.ds(len(hits) - 1, 1), :] = ik
    tot = vals[0]
    for v in vals[1:]:
        tot = tot + v
    onehot = jnp.zeros((n_e, tm), F32)
    for k in range(top_k):
        wts_ref[pl.ds(k, 1), :] = vals[k] / tot * ROUTED_SCALE
        onehot = onehot + jnp.where(hits[k], 1.0, 0.0)
    r = lax.broadcasted_iota(jnp.int32, (tm, tm), 0)
    c = lax.broadcasted_iota(jnp.int32, (tm, tm), 1)
    upper = jnp.where(r < c, 1.0, 0.0).astype(BF16)
    base = jnp.dot(onehot.astype(BF16), upper, preferred_element_type=F32) + carry_ref[...]
    for k in range(top_k):
        rk = jnp.sum(jnp.where(hits[k], base, 0.0), axis=0, keepdims=True)
        rank_ref[pl.ds(k, 1), :] = rk.astype(jnp.int32)
    carry_ref[...] = carry_ref[...] + jnp.sum(onehot, axis=1, keepdims=True)
    cnt_ref[...] = carry_ref[...]


def _router(x2, g, mod3, seq, w_router, router_bias):
    n, d = x2.shape
    n_e = w_router.shape[1]
    tm = _tile(seq, 256)
    per_b = seq // tm
    wr_hi = w_router.T.astype(BF16)
    wr_lo = (w_router.T - wr_hi.astype(F32)).astype(BF16)
    wr_t = jnp.stack([wr_hi, wr_lo])
    return pl.pallas_call(
        functools.partial(_router_kernel, top_k=TOP_K),
        out_shape=[jax.ShapeDtypeStruct((n, d // 2), U32),
                   jax.ShapeDtypeStruct((TOP_K, n), jnp.int32),
                   jax.ShapeDtypeStruct((TOP_K, n), F32),
                   jax.ShapeDtypeStruct((TOP_K, n), jnp.int32),
                   jax.ShapeDtypeStruct((n_e, 1), F32)],
        grid=(n // tm,),
        in_specs=[pl.BlockSpec((tm, d), lambda i: (i, 0)),
                  pl.BlockSpec((1, d), lambda i: (0, 0)),
                  pl.BlockSpec((1, N_MOD, d), lambda i: (i // per_b, 0, 0)),
                  pl.BlockSpec((2, n_e, d), lambda i: (0, 0, 0)),
                  pl.BlockSpec((n_e, 1), lambda i: (0, 0))],
        out_specs=[pl.BlockSpec((tm, d // 2), lambda i: (i, 0)),
                   pl.BlockSpec((TOP_K, tm), lambda i: (0, i)),
                   pl.BlockSpec((TOP_K, tm), lambda i: (0, i)),
                   pl.BlockSpec((TOP_K, tm), lambda i: (0, i)),
                   pl.BlockSpec((n_e, 1), lambda i: (0, 0))],
        scratch_shapes=[pltpu.VMEM((n_e, 1), F32)],
        compiler_params=_params(("arbitrary",)),
        name="moe_router",
    )(x2, g.reshape(1, d), mod3, wr_t, router_bias.reshape(n_e, 1))


def _dispatch_kernel(cnt_ref, pstart_ref, nu_ref, dest_ref, h_ref, xs_ref, zero_ref, sem, zsem,
                     *, top_k, n_e, rows):
    i = pl.program_id(0)
    tm = h_ref.shape[0]
    n_blocks = xs_ref.shape[0] // rows

    def issue(r, carry):
        for k in range(top_k):
            pltpu.make_async_copy(h_ref.at[pl.ds(r, 1), :],
                                  xs_ref.at[pl.ds(dest_ref[0, k, r], 1), :], sem).start(priority=k % 2)
        return carry

    lax.fori_loop(0, tm, issue, 0)

    @pl.when(i == 0)
    def _():
        zero_ref[...] = jnp.zeros_like(zero_ref)

        def zero_fill(wait):
            def go(copy):
                if wait:
                    copy.wait()
                else:
                    copy.start()

            def per_expert(e, carry):
                cnt = cnt_ref[e]
                pad = (rows - cnt % rows) % rows
                first = pstart_ref[e] + cnt

                def one_row(r, c):
                    go(pltpu.make_async_copy(zero_ref.at[pl.ds(0, 1), :],
                                             xs_ref.at[pl.ds(first + r, 1), :], zsem))
                    return c

                lax.fori_loop(0, pad % SUBLANES, one_row, 0)
                end = first + pad
                size = rows // 2
                while size >= SUBLANES:
                    end = end - (pad & size)

                    @pl.when((pad & size) != 0)
                    def _(size=size, end=end):
                        go(pltpu.make_async_copy(zero_ref.at[pl.ds(0, size), :],
                                                 xs_ref.at[pl.ds(pl.multiple_of(end, SUBLANES), size), :],
                                                 zsem))
                    size //= 2
                return carry

            lax.fori_loop(0, n_e, per_expert, 0)

            def per_block(b, carry):
                go(pltpu.make_async_copy(zero_ref, xs_ref.at[pl.ds(pl.multiple_of(b * rows, rows), rows), :],
                                         zsem))
                return carry

            lax.fori_loop(nu_ref[0], n_blocks, per_block, 0)

        zero_fill(False)
        zero_fill(True)

    for k in range(top_k):
        pltpu.make_async_copy(h_ref, xs_ref.at[pl.ds(0, tm), :], sem).wait()


def _dispatch(counts, pstart, n_used, dest3, hp, n_slots):
    n, w = hp.shape
    n_tiles, top_k, tm = dest3.shape
    n_e = counts.shape[0]
    return pl.pallas_call(
        functools.partial(_dispatch_kernel, top_k=top_k, n_e=n_e, rows=MOE_ROWS),
        out_shape=jax.ShapeDtypeStruct((n_slots, w), U32),
        grid_spec=pltpu.PrefetchScalarGridSpec(
            num_scalar_prefetch=3,
            grid=(n_tiles,),
            in_specs=[pl.BlockSpec((1, top_k, tm), lambda i, c, p, u: (i, 0, 0),
                                   memory_space=pltpu.SMEM),
                      pl.BlockSpec((tm, w), lambda i, c, p, u: (i, 0))],
            out_specs=pl.BlockSpec(memory_space=pl.ANY),
            scratch_shapes=[pltpu.VMEM((MOE_ROWS, w), U32), pltpu.SemaphoreType.DMA(()),
                            pltpu.SemaphoreType.DMA(())],
        ),
        compiler_params=_params(("arbitrary",)),
        name="moe_dispatch",
    )(counts, pstart, n_used, dest3, hp)


def _swiglu_packed(xp, wg, wu, wd):
    xh, xl = _unpack_halves(xp)
    half = xh.shape[1]
    gate = (jnp.dot(xh, wg[:half], preferred_element_type=F32)
            + jnp.dot(xl, wg[half:], preferred_element_type=F32))
    up = (jnp.dot(xh, wu[:half], preferred_element_type=F32)
          + jnp.dot(xl, wu[half:], preferred_element_type=F32))
    hm = (gate * jax.nn.sigmoid(gate) * up).astype(BF16)
    return jnp.dot(hm, wd, preferred_element_type=F32)


def _expert_kernel(be_ref, nu_ref, x_ref, wg_ref, wu_ref, wd_ref, o_ref):
    b = pl.program_id(0)

    @pl.when(b < nu_ref[0])
    def _():
        y = _swiglu_packed(x_ref[...], wg_ref[0], wu_ref[0], wd_ref[0])
        o_ref[...] = _pack_halves(y)

    @pl.when(b >= nu_ref[0])
    def _():
        o_ref[...] = jnp.zeros_like(o_ref)


def _experts(block_e, n_used, xs, wg, wu, wd):
    n_slots, w = xs.shape
    n_e, d, de = wg.shape
    rows = MOE_ROWS
    n_blocks = n_slots // rows
    return pl.pallas_call(
        _expert_kernel,
        out_shape=jax.ShapeDtypeStruct((n_slots, w), U32),
        grid_spec=pltpu.PrefetchScalarGridSpec(
            num_scalar_prefetch=2,
            grid=(n_blocks,),
            in_specs=[pl.BlockSpec((rows, w), lambda b, be, nu: (jnp.minimum(b, nu[0] - 1), 0)),
                      pl.BlockSpec((1, d, de), lambda b, be, nu: (be[b], 0, 0)),
                      pl.BlockSpec((1, d, de), lambda b, be, nu: (be[b], 0, 0)),
                      pl.BlockSpec((1, de, d), lambda b, be, nu: (be[b], 0, 0))],
            out_specs=pl.BlockSpec((rows, w), lambda b, be, nu: (b, 0)),
        ),
        compiler_params=_params(("arbitrary",)),
        name="moe_experts",
    )(block_e, n_used, xs, wg, wu, wd)


def _shared_kernel(x_ref, wg_ref, wu_ref, wd_ref, o_ref):
    o_ref[...] = _swiglu_packed(x_ref[...], wg_ref[...], wu_ref[...], wd_ref[...])


def _shared_expert(hp, wg, wu, wd):
    n, w = hp.shape
    d, ds = wg.shape
    tm = _tile(n, 512)
    return pl.pallas_call(
        _shared_kernel,
        out_shape=jax.ShapeDtypeStruct((n, d), F32),
        grid=(n // tm,),
        in_specs=[pl.BlockSpec((tm, w), lambda i: (i, 0)),
                  pl.BlockSpec((d, ds), lambda i: (0, 0)),
                  pl.BlockSpec((d, ds), lambda i: (0, 0)),
                  pl.BlockSpec((ds, d), lambda i: (0, 0))],
        out_specs=pl.BlockSpec((tm, d), lambda i: (i, 0)),
        compiler_params=_params(("arbitrary",)),
        name="moe_shared",
    )(hp, wg, wu, wd)


def _combine_kernel(dest_ref, next_ref, w_ref, x_ref, sh_ref, mod_ref, g_ref, ys_ref, o_ref, buf_ref, sems,
                    *, top_k):
    tm, d = x_ref.shape
    half = d // 2
    i = pl.program_id(0)
    slot = i % 2

    def gather(idx_ref, slot):
        def issue(r, carry):
            for k in range(top_k):
                pltpu.make_async_copy(ys_ref.at[pl.ds(idx_ref[0, k, r], 1), :],
                                      buf_ref.at[slot, k, pl.ds(r, 1), :],
                                      sems.at[slot]).start(priority=k % 2)
            return carry
        lax.fori_loop(0, tm, issue, 0)

    @pl.when(i == 0)
    def _():
        gather(dest_ref, 0)

    @pl.when(i + 1 < pl.num_programs(0))
    def _():
        gather(next_ref, 1 - slot)

    for k in range(top_k):
        pltpu.make_async_copy(ys_ref.at[pl.ds(0, tm), :], buf_ref.at[slot, k], sems.at[slot]).wait()

    acc_h = jnp.zeros((tm, half), F32)
    acc_l = jnp.zeros((tm, half), F32)
    for k in range(top_k):
        p = buf_ref[slot, k]
        wk = w_ref[:, k:k + 1]
        acc_h = acc_h + wk * pltpu.bitcast(p & jnp.uint32(HI_MASK), F32)
        acc_l = acc_l + wk * pltpu.bitcast(p << 16, F32)
    gate = mod_ref[0, 5:6, :]
    x_h = x_ref[:, :half] + gate[:, :half] * (acc_h + sh_ref[:, :half])
    x_l = x_ref[:, half:] + gate[:, half:] * (acc_l + sh_ref[:, half:])
    ms = (jnp.sum(x_h * x_h, axis=1, keepdims=True) + jnp.sum(x_l * x_l, axis=1, keepdims=True)) / d
    inv = lax.rsqrt(ms + NORM_EPS)
    o_ref[:, :half] = x_h * inv * g_ref[:, :half]
    o_ref[:, half:] = x_l * inv * g_ref[:, half:]


def _combine(dest3, wts_t, x2, shared, mod3, g_final, ys, seq):
    n, d = x2.shape
    n_tiles, top_k, tm = dest3.shape
    per_b = seq // tm
    return pl.pallas_call(
        functools.partial(_combine_kernel, top_k=top_k),
        out_shape=jax.ShapeDtypeStruct((n, d), F32),
        grid=(n_tiles,),
        in_specs=[pl.BlockSpec((1, top_k, tm), lambda i: (i, 0, 0), memory_space=pltpu.SMEM),
                  pl.BlockSpec((1, top_k, tm), lambda i: (jnp.minimum(i + 1, n_tiles - 1), 0, 0),
                               memory_space=pltpu.SMEM),
                  pl.BlockSpec((tm, top_k), lambda i: (i, 0)),
                  pl.BlockSpec((tm, d), lambda i: (i, 0)),
                  pl.BlockSpec((tm, d), lambda i: (i, 0)),
                  pl.BlockSpec((1, N_MOD, d), lambda i: (i // per_b, 0, 0)),
                  pl.BlockSpec((1, d), lambda i: (0, 0)),
                  pl.BlockSpec(memory_space=pl.ANY)],
        out_specs=pl.BlockSpec((tm, d), lambda i: (i, 0)),
        scratch_shapes=[pltpu.VMEM((2, top_k, tm, d // 2), U32), pltpu.SemaphoreType.DMA((2,))],
        compiler_params=_params(("arbitrary",)),
        name="moe_combine",
    )(dest3, dest3, wts_t, x2, shared, mod3, g_final.reshape(1, d), ys)


def _retile(a, tm):
    k, n = a.shape
    return a.reshape(k, n // tm, tm).transpose(1, 0, 2)


def kernel(x, c, w_ada, b_ada, g_mix, w_in, b_forget, lam_re, lam_im, log_dt, b_re, b_im, c_re, c_im, d_skip, w_glu, w_proj_attn, w_proj_ssm, w_out, g_ffn, w_router, router_bias, w_gate_e, w_up_e, w_down_e, w_gate_s, w_up_s, w_down_s, g_final):
    bsz, seq, d = x.shape
    n = bsz * seq
    depth = w_ada.shape[0]
    n_heads = b_forget.shape[1]
    attn_w = w_proj_attn.shape[1]
    dh = attn_w // n_heads
    ssm_w = w_proj_ssm.shape[1]
    n_e = w_router.shape[2]

    assert depth == 1, "the final rms_norm is fused into the last layer's combine kernel"
    c_pad = jnp.zeros((8, d), F32).at[:bsz].set(c)
    x2 = x.reshape(n, d)
    out = None
    for l in range(depth):
        mod = _ada(c_pad, w_ada[l], b_ada[l])
        mod3 = mod[:bsz].reshape(bsz, N_MOD, d)

        h = _norm_modulate(x2, g_mix[l], mod3, seq, 0, 1)
        w = w_in[l]
        o_f = 3 * attn_w
        o_u = o_f + n_heads
        o_ga = o_u + ssm_w
        w_qkv = w[:, :o_f].astype(BF16)
        q_scale = jnp.concatenate([jnp.full((attn_w,), dh ** -0.5 * LOG2E, F32),
                                   jnp.ones((2 * attn_w,), F32)]).reshape(1, o_f)
        qkv = _mm([h], [(0, w_qkv, 0)], [_row_extra(q_scale)],
                  lambda accs, ex: accs[0] * ex[0][...], o_f, BF16, "proj_qkv")
        w_f = jnp.zeros((d, LANES), F32).at[:, :n_heads].set(w[:, o_f:o_u]).astype(BF16)
        f_logit = _mm([h], [(0, w_f, 0)], [], lambda accs, ex: accs[0], LANES, F32, "proj_forget")
        u = _mm([h], [(0, w[:, o_u:o_ga].astype(BF16), 0)], [], lambda accs, ex: accs[0],
                ssm_w, F32, "proj_u")
        gates = _mm([h], [(0, w[:, o_ga:].astype(BF16), 0)], [],
                    lambda accs, ex: jax.nn.sigmoid(accs[0]), 2 * d, BF16, "proj_gates")

        b_row = jnp.zeros((1, LANES), F32).at[0, :n_heads].set(b_forget[l])
        cum = _forget_cumsum(f_logit.reshape(bsz, seq, LANES), b_row)[:, :, :n_heads]
        attn, (wg_e, wu_e, wd_e) = _attention(qkv.reshape(bsz, seq, o_f), cum, n_heads, dh,
                                              side=(w_gate_e[l], w_up_e[l], w_down_e[l]))
        attn = attn.reshape(n, attn_w)

        sg = _s5_ssm(u, seq, lam_re[l], lam_im[l], log_dt[l], b_re[l], b_im[l],
                     c_re[l], c_im[l], d_skip[l])
        w_g = w_glu[l].astype(BF16)
        ssm = _mm([sg], [(0, w_g, 0), (0, w_g, ssm_w)], [],
                  lambda accs, ex: accs[0] * jax.nn.sigmoid(accs[1]), ssm_w, BF16, "ssm_glu")

        merged = _mm([attn, ssm],
                     [(0, w_proj_attn[l].astype(BF16), 0), (1, w_proj_ssm[l].astype(BF16), 0)],
                     [_tile_extra(gates), _tile_extra(gates, d)],
                     lambda accs, ex: (ex[0][...].astype(F32) * accs[0]
                                       + ex[1][...].astype(F32) * accs[1]),
                     d, BF16, "merge_proj")
        x2 = _mm([merged], [(0, w_out[l].astype(BF16), 0)], [_tile_extra(x2), _mod_extra(mod3, seq)],
                 lambda accs, ex: ex[0][...] + ex[1][0, 2:3, :] * accs[0], d, F32, "out_proj",
                 row_unit=seq)

        hp, idx_t, wts_t, rank_t, cnt = _router(x2, g_ffn[l], mod3, seq, w_router[l], router_bias[l])
        counts = cnt[:, 0].astype(jnp.int32)
        padded = (counts + MOE_ROWS - 1) // MOE_ROWS * MOE_ROWS
        pend = jnp.cumsum(padded)
        pstart = pend - padded
        e_ids = jnp.arange(n_e, dtype=jnp.int32)
        first = jnp.sum(jnp.where(idx_t[None] == e_ids[:, None, None], pstart[:, None, None], 0), axis=0)
        dest = first + rank_t
        n_blocks = -(-(n * TOP_K + n_e * (MOE_ROWS - 1)) // MOE_ROWS)
        n_used = (pend[-1] // MOE_ROWS).astype(jnp.int32).reshape(1)
        blk = jnp.minimum(jnp.arange(n_blocks, dtype=jnp.int32), n_used - 1)
        block_e = jnp.minimum(jnp.sum((blk * MOE_ROWS)[:, None] >= pend[None, :], axis=1),
                              n_e - 1).astype(jnp.int32)
        tm_d = _tile(seq, 256)
        xs = _dispatch(counts, pstart.astype(jnp.int32), n_used, _retile(dest, tm_d), hp,
                       n_blocks * MOE_ROWS)
        ys = _experts(block_e, n_used, xs, wg_e, wu_e, wd_e)
        shared = _shared_expert(hp, w_gate_s[l].astype(BF16), w_up_s[l].astype(BF16),
                                w_down_s[l].astype(BF16))
        tm_c = _tile(seq, 128)
        out = _combine(_retile(dest, tm_c), wts_t.T, x2, shared, mod3, g_final, ys, seq)
    return out.reshape(bsz, seq, d)
```

```python
import functools
import math

import jax
import jax.numpy as jnp
from jax import lax
from jax.experimental import pallas as pl
from jax.experimental.pallas import tpu as pltpu

TOP_K = 8
ROUTED_SCALE = 2.5
NORM_EPS = 1e-6
N_MOD = 6
SSM_SUB = 8
SSM_ROWS = 256
MOE_ROWS = 256
ATTN_SUB_ROWS = 512
LOG2E = 1.4426950408889634

LANES = 128
SUBLANES = 8
VMEM_LIMIT = 56 * 1024 * 1024

F32 = jnp.float32
BF16 = jnp.bfloat16
U32 = jnp.uint32
HI_MASK = 0xFFFF0000


def _tile(n, pref):
    if n <= pref:
        return n
    t = pref
    while n % t:
        t //= 2
    return t


def _params(sem):
    return pltpu.CompilerParams(dimension_semantics=sem, vmem_limit_bytes=VMEM_LIMIT)


def _pack_halves(y):
    w = y.shape[1] // 2
    bits = pltpu.bitcast(y.astype(BF16).astype(F32), U32)
    return bits[:, :w] | (bits[:, w:] >> 16)


def _unpack_halves(p):
    hi = pltpu.bitcast(p & jnp.uint32(HI_MASK), F32)
    lo = pltpu.bitcast(p << 16, F32)
    return hi.astype(BF16), lo.astype(BF16)


def _ada_kernel(c_ref, w_ref, b_ref, o_ref):
    c = c_ref[...]
    cond = c * jax.nn.sigmoid(c)
    o_ref[...] = jnp.dot(cond.astype(BF16), w_ref[...].astype(BF16),
                         preferred_element_type=F32) + b_ref[...]


def _ada(c_pad, w_ada, b_ada):
    rows, d = c_pad.shape
    n = w_ada.shape[1]
    tn = _tile(n, 512)
    return pl.pallas_call(
        _ada_kernel,
        out_shape=jax.ShapeDtypeStruct((rows, n), F32),
        grid=(n // tn,),
        in_specs=[pl.BlockSpec((rows, d), lambda j: (0, 0)),
                  pl.BlockSpec((d, tn), lambda j: (0, j)),
                  pl.BlockSpec((1, tn), lambda j: (0, j))],
        out_specs=pl.BlockSpec((rows, tn), lambda j: (0, j)),
        compiler_params=_params(("arbitrary",)),
        name="ada_mod",
    )(c_pad, w_ada, b_ada.reshape(1, n))


def _norm_mod(x, g_row, mod_ref, shift_row, scale_row):
    ms = jnp.mean(x * x, axis=-1, keepdims=True)
    y = x * lax.rsqrt(ms + NORM_EPS) * g_row
    sc = mod_ref[0, scale_row:scale_row + 1, :]
    sh = mod_ref[0, shift_row:shift_row + 1, :]
    return y * (1.0 + sc) + sh


def _norm_mod_kernel(x_ref, g_ref, mod_ref, o_ref, *, shift_row, scale_row):
    o_ref[...] = _norm_mod(x_ref[...], g_ref[...], mod_ref, shift_row, scale_row).astype(o_ref.dtype)


def _norm_modulate(x2, g, mod3, seq, shift_row, scale_row):
    n, d = x2.shape
    tm = _tile(seq, 256)
    per_b = seq // tm
    return pl.pallas_call(
        functools.partial(_norm_mod_kernel, shift_row=shift_row, scale_row=scale_row),
        out_shape=jax.ShapeDtypeStruct((n, d), BF16),
        grid=(n // tm,),
        in_specs=[pl.BlockSpec((tm, d), lambda i: (i, 0)),
                  pl.BlockSpec((1, d), lambda i: (0, 0)),
                  pl.BlockSpec((1, N_MOD, d), lambda i: (i // per_b, 0, 0))],
        out_specs=pl.BlockSpec((tm, d), lambda i: (i, 0)),
        compiler_params=_params(("arbitrary",)),
        name="norm_mod",
    )(x2, g.reshape(1, d), mod3)


def _mm_kernel(*refs, n_a, a_of_b, epilogue):
    o_ref = refs[-1]
    n_b = len(a_of_b)
    lhs = [refs[a][...].astype(BF16) for a in range(n_a)]
    accs = [jnp.dot(lhs[a_of_b[p]], refs[n_a + p][...], preferred_element_type=F32)
            for p in range(n_b)]
    extras = refs[n_a + n_b:-1]
    o_ref[...] = epilogue(accs, extras).astype(o_ref.dtype)


def _mm(a_list, b_list, extras, epilogue, n_out, out_dtype, name, row_unit=None,
        tm_pref=1024, tn_pref=1024):
    m = a_list[0].shape[0]
    tm = _tile(row_unit or m, tm_pref)
    tn = _tile(n_out, tn_pref)
    in_specs, args = [], []
    for a in a_list:
        in_specs.append(pl.BlockSpec((tm, a.shape[1]), lambda i, j: (i, 0)))
        args.append(a)
    for _, b, col in b_list:
        assert col % tn == 0, "weight sections must start on a column-tile boundary"
        in_specs.append(pl.BlockSpec((b.shape[0], tn), lambda i, j, off=col // tn: (0, j + off)))
        args.append(b)
    for arr, bshape, imap in extras:
        in_specs.append(pl.BlockSpec(bshape(tm, tn), imap(tm, tn)))
        args.append(arr)
    return pl.pallas_call(
        functools.partial(_mm_kernel, n_a=len(a_list), a_of_b=tuple(ai for ai, _, _ in b_list),
                          epilogue=epilogue),
        out_shape=jax.ShapeDtypeStruct((m, n_out), out_dtype),
        grid=(m // tm, n_out // tn),
        in_specs=in_specs,
        out_specs=pl.BlockSpec((tm, tn), lambda i, j: (i, j)),
        compiler_params=_params(("arbitrary", "arbitrary")),
        name=name,
    )(*args)


def _row_extra(row):
    return (row, lambda tm, tn: (1, tn), lambda tm, tn: (lambda i, j: (0, j)))


def _tile_extra(arr, col=0):
    return (arr, lambda tm, tn: (tm, tn), lambda tm, tn: (lambda i, j: (i, j + col // tn)))


def _mod_extra(mod3, seq):
    return (mod3, lambda tm, tn: (1, N_MOD, tn),
            lambda tm, tn: (lambda i, j: (i // (seq // tm), 0, j)))


def _cum_kernel(f_ref, b_ref, o_ref, *, chunk):
    seq = f_ref.shape[1]
    x = f_ref[0] + b_ref[...]
    lf = jnp.minimum(x, 0.0) - jnp.log(1.0 + jnp.exp(-jnp.abs(x)))
    o_ref[0] = lf
    r = lax.broadcasted_iota(jnp.int32, (chunk, chunk), 0)
    c = lax.broadcasted_iota(jnp.int32, (chunk, chunk), 1)
    tri = jnp.where(c <= r, 1.0, 0.0).astype(F32)

    def body(i, carry):
        start = pl.multiple_of(i * chunk, chunk)
        blk = o_ref[0, pl.ds(start, chunk), :]
        cs = jnp.dot(tri, blk, precision=lax.Precision.HIGHEST,
                     preferred_element_type=F32) + carry
        o_ref[0, pl.ds(start, chunk), :] = cs
        return cs[chunk - 1:chunk, :]

    lax.fori_loop(0, seq // chunk, body, jnp.zeros((1, f_ref.shape[2]), F32))


def _forget_cumsum(f3, b_row):
    bsz, seq, w = f3.shape
    chunk = _tile(seq, 128)
    return pl.pallas_call(
        functools.partial(_cum_kernel, chunk=chunk),
        out_shape=jax.ShapeDtypeStruct((bsz, seq, w), F32),
        grid=(bsz,),
        in_specs=[pl.BlockSpec((1, seq, w), lambda b: (b, 0, 0)),
                  pl.BlockSpec((1, w), lambda b: (0, 0))],
        out_specs=pl.BlockSpec((1, seq, w), lambda b: (b, 0, 0)),
        compiler_params=_params(("arbitrary",)),
        name="forget_cumsum",
    )(f3, b_row)


def _attn_kernel(c0_ref, *refs, blk, n_sub, n_heads, n_q, n_side, paired):
    n_qb = 2 if paired else 1
    q_refs = refs[:n_qb]
    k_ref, v_ref, ck_ref = refs[n_qb:n_qb + 3]
    side_in = refs[n_qb + 3:n_qb + 3 + n_side]
    o_refs = refs[n_qb + 3 + n_side:2 * n_qb + 3 + n_side]
    side_out = refs[2 * n_qb + 3 + n_side:2 * n_qb + 3 + 2 * n_side]
    sa_ref, sb_ref = refs[2 * n_qb + 3 + 2 * n_side:]
    b = pl.program_id(0)
    h = pl.program_id(1)
    step = pl.program_id(2)
    for w_ref, wo_ref in zip(side_in, side_out):
        wo_ref[...] = w_ref[...].astype(wo_ref.dtype)
    blocks = [step, n_q - 1 - step] if paired else [step]
    for i, q_ref, o_ref in zip(blocks, q_refs, o_refs):
        _attn_sweep(i, c0_ref[(b * n_heads + h) * n_q + i], q_ref, k_ref, v_ref, ck_ref, o_ref,
                    sa_ref, sb_ref, blk=blk, n_sub=n_sub)


def _attn_sweep(i, c0, q_ref, k_ref, v_ref, ck_ref, o_ref, sa_ref, sb_ref, *, blk, n_sub):
    sub = blk // n_sub
    qs = [q_ref[0, a * sub:(a + 1) * sub, :] for a in range(n_sub)]

    def logits(j, s_ref):
        kb = k_ref[0, pl.ds(pl.multiple_of(j * blk, blk), blk), :]
        bias = (c0 - ck_ref[0, 0, j]) * LOG2E
        for a, q in enumerate(qs):
            s_ref[a * sub:(a + 1) * sub, :] = lax.dot_general(
                q, kb, (((1,), (1,)), ((), ())), preferred_element_type=F32) + bias

    def update(j, s_ref, state, diagonal=False):
        vb = v_ref[0, pl.ds(pl.multiple_of(j * blk, blk), blk), :]
        dh = vb.shape[1]
        vb = jnp.concatenate([vb, jnp.ones_like(vb)], axis=1)
        out = []
        for a, (m, l, acc) in enumerate(state):
            t = s_ref[a * sub:(a + 1) * sub, :]
            if diagonal:
                row = lax.broadcasted_iota(jnp.int32, (sub, blk), 0)
                col = lax.broadcasted_iota(jnp.int32, (sub, blk), 1)
                t = jnp.where(col <= row + a * sub, t, -jnp.inf)
            m_new = jnp.maximum(m, jnp.max(t, axis=1, keepdims=True))
            alpha = jnp.exp2(m - m_new)
            p = jnp.exp2(t - m_new)
            pv = jnp.dot(p.astype(BF16), vb, preferred_element_type=F32)
            l = alpha * l + pv[:, dh:dh + 1]
            acc = alpha * acc + pv[:, :dh]
            out.append((m_new, l, acc))
        return tuple(out)

    def body(pair, state):
        j = 2 * pair
        logits(j + 1, sb_ref)
        state = update(j, sa_ref, state)
        logits(j + 2, sa_ref)
        return update(j + 1, sb_ref, state)

    def body2(quad, state):
        return body(2 * quad + 1, body(2 * quad, state))

    init = tuple((jnp.full((sub, 1), -jnp.inf, F32), jnp.zeros((sub, 1), F32),
                  jnp.zeros((sub, q_ref.shape[2]), F32)) for _ in range(n_sub))
    logits(0, sa_ref)
    state = lax.fori_loop(0, i // 4, body2, init)
    state = lax.fori_loop(2 * (i // 4), i // 2, body, state)

    def finish(state):
        for a, (m, l, acc) in enumerate(state):
            o_ref[0, a * sub:(a + 1) * sub, :] = (acc / l).astype(o_ref.dtype)

    @pl.when(i % 2 == 0)
    def _():
        finish(update(i, sa_ref, state, diagonal=True))

    @pl.when(i % 2 == 1)
    def _():
        logits(i, sb_ref)
        finish(update(i, sb_ref, update(i - 1, sa_ref, state), diagonal=True))


def _attention(qkv3, cum, n_heads, dh, side=()):
    bsz, seq, _ = qkv3.shape
    blk = _tile(seq, 512)
    n_q = seq // blk
    paired = n_q % 2 == 0
    n_qs = n_q // 2 if paired else n_q
    n_steps = bsz * n_heads * n_qs
    cum_t = cum.transpose(0, 2, 1)
    ck = cum_t.reshape(bsz, n_heads, n_q, 1, blk)
    c0 = cum_t[:, :, ::blk].reshape(-1)
    side2 = [w.reshape(-1, w.shape[-1]) for w in side]
    bf16_rows = 2 * SUBLANES
    assert all(w.shape[0] % (n_steps * bf16_rows) == 0 for w in side2)

    def slab(w):
        return pl.BlockSpec((w.shape[0] // n_steps, w.shape[1]),
                            lambda b, h, i, c: ((b * n_heads + h) * n_qs + i, 0))

    q_specs = [pl.BlockSpec((1, blk, dh), lambda b, h, i, c: (b, i, h))]
    o_specs = [pl.BlockSpec((1, blk, dh), lambda b, h, i, c: (b, i, h))]
    o_rows = [seq]
    if paired:
        q_specs.append(pl.BlockSpec((1, blk, dh), lambda b, h, i, c: (b, n_q - 1 - i, h)))
        o_specs.append(pl.BlockSpec((1, blk, dh), lambda b, h, i, c: (b, n_qs - 1 - i, h)))
        o_rows = [seq // 2, seq // 2]
    n_o = len(o_specs)
    out = pl.pallas_call(
        functools.partial(_attn_kernel, blk=blk, n_sub=max(1, blk // ATTN_SUB_ROWS),
                          n_heads=n_heads, n_q=n_q, n_side=len(side2), paired=paired),
        out_shape=[jax.ShapeDtypeStruct((bsz, r, n_heads * dh), BF16) for r in o_rows]
        + [jax.ShapeDtypeStruct(w.shape, BF16) for w in side2],
        grid_spec=pltpu.PrefetchScalarGridSpec(
            num_scalar_prefetch=1,
            grid=(bsz, n_heads, n_qs),
            in_specs=q_specs
            + [pl.BlockSpec((1, seq, dh), lambda b, h, i, c: (b, 0, n_heads + h)),
               pl.BlockSpec((1, seq, dh), lambda b, h, i, c: (b, 0, 2 * n_heads + h)),
               pl.BlockSpec((1, 1, n_q, 1, blk), lambda b, h, i, c: (b, h, 0, 0, 0))]
            + [slab(w) for w in side2],
            out_specs=o_specs + [slab(w) for w in side2],
            scratch_shapes=[pltpu.VMEM((blk, blk), F32), pltpu.VMEM((blk, blk), F32)],
        ),
        compiler_params=_params(("arbitrary", "arbitrary", "arbitrary")),
        name="fox_attention",
    )(c0, *([qkv3] * n_o), qkv3, qkv3, ck, *side2)
    attn = out[0] if n_o == 1 else jnp.concatenate(out[:n_o], axis=1)
    return attn, [o.reshape(w.shape) for o, w in zip(out[n_o:], side)]


def _gelu_tanh(x):
    return 0.5 * x * (1.0 + jnp.tanh(math.sqrt(2.0 / math.pi) * (x + 0.044715 * (x * x * x))))


def _ssm_kernel(u_ref, wm_ref, wz_ref, wy_ref, apow_ref, dd_ref, o_ref, xcat_ref, carry_ref,
                *, sub, n_levels):
    rb = u_ref.shape[0] // sub

    @pl.when(pl.program_id(2) == 0)
    def _():
        carry_ref[...] = jnp.zeros_like(carry_ref)

    for i in range(sub):
        xcat_ref[:, i * LANES:(i + 1) * LANES] = u_ref[pl.ds(i, rb, stride=sub), :].astype(BF16)
    x = xcat_ref[...]
    y = jnp.dot(x, wm_ref[0], preferred_element_type=F32)
    s = jnp.dot(x, wz_ref[0], preferred_element_type=F32)
    half = s.shape[1] // 2
    row = lax.broadcasted_iota(jnp.int32, s.shape, 0)
    carry = carry_ref[...]

    a1 = apow_ref[0, 0:1, :]
    a2 = apow_ref[0, 1:2, :]

    def times_a(v):
        return a1 * v + a2 * pltpu.roll(v, half, axis=1)

    s = s + jnp.where(row == 0, times_a(carry), 0.0)
    for lvl in range(n_levels):
        shift = 1 << lvl
        s = s + times_a(jnp.where(row >= shift, pltpu.roll(s, shift, axis=0), 0.0))
        a1, a2 = a1 * a1 - a2 * a2, 2.0 * a1 * a2
    s_in = jnp.where(row >= 1, pltpu.roll(s, 1, axis=0), carry)
    carry_ref[...] = s[rb - 1:rb, :]
    y = y + jnp.dot(s_in.astype(BF16), wy_ref[0], preferred_element_type=F32)
    for j in range(sub):
        yj = y[:, j * LANES:(j + 1) * LANES] + dd_ref[0] * u_ref[pl.ds(j, rb, stride=sub), :]
        o_ref[pl.ds(j, rb, stride=sub), :] = _gelu_tanh(yj)


def _ssm_tables(lam_re, lam_im, log_dt, b_re, b_im, c_re, c_im, d_skip):
    g, p = lam_re.shape
    hg = b_re.shape[-1]
    sub = SSM_SUB
    dt = jnp.exp(log_dt)[:, None]
    mag = jnp.exp(lam_re * dt)
    a_re = mag * jnp.cos(lam_im * dt)
    a_im = mag * jnp.sin(lam_im * dt)
    den = lam_re * lam_re + lam_im * lam_im
    k_re = ((a_re - 1) * lam_re + a_im * lam_im) / den
    k_im = (a_im * lam_re - (a_re - 1) * lam_im) / den
    bb_re = k_re[..., None] * b_re - k_im[..., None] * b_im
    bb_im = k_re[..., None] * b_im + k_im[..., None] * b_re

    tau = jnp.arange(sub + 1, dtype=F32)[:, None, None]
    pmag = jnp.exp(tau * (lam_re * dt))
    pw_re = pmag * jnp.cos(tau * (lam_im * dt))
    pw_im = pmag * jnp.sin(tau * (lam_im * dt))

    ca_re = c_re[None] * pw_re[:sub, :, None, :] - c_im[None] * pw_im[:sub, :, None, :]
    ca_im = c_re[None] * pw_im[:sub, :, None, :] + c_im[None] * pw_re[:sub, :, None, :]
    kk = (jnp.einsum('tgop,gpi->tgio', ca_re, bb_re, precision=lax.Precision.HIGHEST)
          - jnp.einsum('tgop,gpi->tgio', ca_im, bb_im, precision=lax.Precision.HIGHEST))

    gt = LANES // hg
    nt = g // gt
    wide = sub * LANES
    sw = gt * 2 * p

    def block_diag(x, rows_per_group, cols_per_group):
        n_r, n_c = gt * rows_per_group, gt * cols_per_group
        r = lax.broadcasted_iota(jnp.int32, (n_r, n_c), 0) // rows_per_group
        c = lax.broadcasted_iota(jnp.int32, (n_r, n_c), 1) // cols_per_group
        return jnp.where(r == c, jnp.tile(x, (1,) * (x.ndim - 1) + (gt,)), 0.0)

    bd = block_diag(kk.reshape(sub, nt, LANES, hg), hg, hg).astype(BF16)
    zero = jnp.zeros_like(bd[0])
    wm = jnp.stack([jnp.concatenate([zero] * i + [bd[k] for k in range(sub - i)], axis=-1)
                    for i in range(sub)], axis=1).reshape(nt, wide, wide)

    rev_re = pw_re[:sub][::-1]
    rev_im = pw_im[:sub][::-1]
    z_re = rev_re[..., None] * bb_re[None] - rev_im[..., None] * bb_im[None]
    z_im = rev_re[..., None] * bb_im[None] + rev_im[..., None] * bb_re[None]
    zz = jnp.stack([z_re, z_im]).transpose(0, 1, 2, 4, 3).reshape(2, sub, nt, LANES, p)
    wz = block_diag(zz, hg, p).astype(BF16)
    wz = wz.transpose(2, 1, 3, 0, 4).reshape(nt, wide, sw)

    cb_re = c_re[None] * pw_re[1:, :, None, :] - c_im[None] * pw_im[1:, :, None, :]
    cb_im = c_re[None] * pw_im[1:, :, None, :] + c_im[None] * pw_re[1:, :, None, :]
    yy = jnp.stack([cb_re, -cb_im]).transpose(0, 1, 2, 4, 3).reshape(2, sub, nt, gt * p, hg)
    wy = block_diag(yy, p, hg).astype(BF16)
    wy = wy.transpose(2, 0, 3, 1, 4).reshape(nt, sw, wide)

    lr, li = pw_re[sub].reshape(nt, gt * p), pw_im[sub].reshape(nt, gt * p)
    apow = jnp.stack([jnp.concatenate([lr, lr], axis=1), jnp.concatenate([-li, li], axis=1)], axis=1)
    apow = jnp.pad(apow, ((0, 0), (0, SUBLANES - 2), (0, 0)))
    dd = d_skip.reshape(nt, 1, LANES)
    return wm, wz, wy, apow, dd


def _s5_ssm(u, seq, lam_re, lam_im, log_dt, b_re, b_im, c_re, c_im, d_skip):
    n, width = u.shape
    g, p = lam_re.shape
    hg = width // g
    sub = SSM_SUB
    assert LANES % hg == 0 and width % LANES == 0
    gt = LANES // hg
    nt = width // LANES
    rb = _tile(seq // sub, SSM_ROWS)
    n_rb = seq // (rb * sub)
    n_levels = max(1, (rb - 1).bit_length())
    wm, wz, wy, apow, dd = _ssm_tables(lam_re, lam_im, log_dt, b_re, b_im, c_re, c_im, d_skip)
    wide = sub * LANES
    sw = gt * 2 * p
    return pl.pallas_call(
        functools.partial(_ssm_kernel, sub=sub, n_levels=n_levels),
        out_shape=jax.ShapeDtypeStruct((n, width), F32),
        grid=(nt, n // seq, n_rb),
        in_specs=[pl.BlockSpec((rb * sub, LANES), lambda t, b, k: (b * n_rb + k, t)),
                  pl.BlockSpec((1, wide, wide), lambda t, b, k: (t, 0, 0)),
                  pl.BlockSpec((1, wide, sw), lambda t, b, k: (t, 0, 0)),
                  pl.BlockSpec((1, sw, wide), lambda t, b, k: (t, 0, 0)),
                  pl.BlockSpec((1, apow.shape[1], sw), lambda t, b, k: (t, 0, 0)),
                  pl.BlockSpec((1, 1, LANES), lambda t, b, k: (t, 0, 0))],
        out_specs=pl.BlockSpec((rb * sub, LANES), lambda t, b, k: (b * n_rb + k, t)),
        scratch_shapes=[pltpu.VMEM((rb, wide), BF16), pltpu.VMEM((1, sw), F32)],
        compiler_params=_params(("arbitrary", "arbitrary", "arbitrary")),
        name="s5_ssm",
    )(u, wm, wz, wy, apow, dd)


def _router_kernel(x_ref, g_ref, mod_ref, wr_ref, rb_ref,
                   hp_ref, idx_ref, wts_ref, rank_ref, cnt_ref, carry_ref, *, top_k):
    i = pl.program_id(0)

    @pl.when(i == 0)
    def _():
        carry_ref[...] = jnp.zeros_like(carry_ref)

    h = _norm_mod(x_ref[...], g_ref[...], mod_ref, 3, 4)
    hp_ref[...] = _pack_halves(h)
    h_hi = h.astype(BF16)
    h_lo = (h - h_hi.astype(F32)).astype(BF16)
    nt = (((1,), (1,)), ((), ()))
    logits = (lax.dot_general(wr_ref[0], h_hi, nt, preferred_element_type=F32)
              + lax.dot_general(wr_ref[0], h_lo, nt, preferred_element_type=F32)
              + lax.dot_general(wr_ref[1], h_hi, nt, preferred_element_type=F32))
    scores = jax.nn.sigmoid(logits)
    sel = scores + rb_ref[...]
    n_e, tm = sel.shape
    e_iota = lax.broadcasted_iota(jnp.int32, (n_e, tm), 0)
    hits, vals = [], []
    for _ in range(top_k):
        mx = jnp.max(sel, axis=0, keepdims=True)
        ik = jnp.min(jnp.where(sel == mx, e_iota, n_e), axis=0, keepdims=True)
        hit = e_iota == ik
        vals.append(jnp.sum(jnp.where(hit, scores, 0.0), axis=0, keepdims=True))
        sel = jnp.where(hit, -jnp.inf, sel)
        hits.append(hit)
        idx_ref[pl.ds(len(hits) - 1, 1), :] = ik
    tot = vals[0]
    for v in vals[1:]:
        tot = tot + v
    onehot = jnp.zeros((n_e, tm), F32)
    for k in range(top_k):
        wts_ref[pl.ds(k, 1), :] = vals[k] / tot * ROUTED_SCALE
        onehot = onehot + jnp.where(hits[k], 1.0, 0.0)
    r = lax.broadcasted_iota(jnp.int32, (tm, tm), 0)
    c = lax.broadcasted_iota(jnp.int32, (tm, tm), 1)
    upper = jnp.where(r < c, 1.0, 0.0).astype(BF16)
    base = jnp.dot(onehot.astype(BF16), upper, preferred_element_type=F32) + carry_ref[...]
    for k in range(top_k):
        rk = jnp.sum(jnp.where(hits[k], base, 0.0), axis=0, keepdims=True)
        rank_ref[pl.ds(k, 1), :] = rk.astype(jnp.int32)
    carry_ref[...] = carry_ref[...] + jnp.sum(onehot, axis=1, keepdims=True)
    cnt_ref[...] = carry_ref[...]


def _router(x2, g, mod3, seq, w_router, router_bias):
    n, d = x2.shape
    n_e = w_router.shape[1]
    tm = _tile(seq, 256)
    per_b = seq // tm
    wr_hi = w_router.T.astype(BF16)
    wr_lo = (w_router.T - wr_hi.astype(F32)).astype(BF16)
    wr_t = jnp.stack([wr_hi, wr_lo])
    return pl.pallas_call(
        functools.partial(_router_kernel, top_k=TOP_K),
        out_shape=[jax.ShapeDtypeStruct((n, d // 2), U32),
                   jax.ShapeDtypeStruct((TOP_K, n), jnp.int32),
                   jax.ShapeDtypeStruct((TOP_K, n), F32),
                   jax.ShapeDtypeStruct((TOP_K, n), jnp.int32),
                   jax.ShapeDtypeStruct((n_e, 1), F32)],
        grid=(n // tm,),
        in_specs=[pl.BlockSpec((tm, d), lambda i: (i, 0)),
                  pl.BlockSpec((1, d), lambda i: (0, 0)),
                  pl.BlockSpec((1, N_MOD, d), lambda i: (i // per_b, 0, 0)),
                  pl.BlockSpec((2, n_e, d), lambda i: (0, 0, 0)),
                  pl.BlockSpec((n_e, 1), lambda i: (0, 0))],
        out_specs=[pl.BlockSpec((tm, d // 2), lambda i: (i, 0)),
                   pl.BlockSpec((TOP_K, tm), lambda i: (0, i)),
                   pl.BlockSpec((TOP_K, tm), lambda i: (0, i)),
                   pl.BlockSpec((TOP_K, tm), lambda i: (0, i)),
                   pl.BlockSpec((n_e, 1), lambda i: (0, 0))],
        scratch_shapes=[pltpu.VMEM((n_e, 1), F32)],
        compiler_params=_params(("arbitrary",)),
        name="moe_router",
    )(x2, g.reshape(1, d), mod3, wr_t, router_bias.reshape(n_e, 1))


def _dispatch_kernel(cnt_ref, pstart_ref, nu_ref, dest_ref, h_ref, xs_ref, zero_ref, sem, zsem,
                     *, top_k, n_e, rows):
    i = pl.program_id(0)
    tm = h_ref.shape[0]
    n_blocks = xs_ref.shape[0] // rows

    def issue(r, carry):
        for k in range(top_k):
            pltpu.make_async_copy(h_ref.at[pl.ds(r, 1), :],
                                  xs_ref.at[pl.ds(dest_ref[0, k, r], 1), :], sem).start(priority=k % 2)
        return carry

    lax.fori_loop(0, tm, issue, 0)

    @pl.when(i == 0)
    def _():
        zero_ref[...] = jnp.zeros_like(zero_ref)

        def zero_fill(wait):
            def go(copy):
                if wait:
                    copy.wait()
                else:
                    copy.start()

            def per_expert(e, carry):
                cnt = cnt_ref[e]
                pad = (rows - cnt % rows) % rows
                first = pstart_ref[e] + cnt

                def one_row(r, c):
                    go(pltpu.make_async_copy(zero_ref.at[pl.ds(0, 1), :],
                                             xs_ref.at[pl.ds(first + r, 1), :], zsem))
                    return c

                lax.fori_loop(0, pad % SUBLANES, one_row, 0)
                end = first + pad
                size = rows // 2
                while size >= SUBLANES:
                    end = end - (pad & size)

                    @pl.when((pad & size) != 0)
                    def _(size=size, end=end):
                        go(pltpu.make_async_copy(zero_ref.at[pl.ds(0, size), :],
                                                 xs_ref.at[pl.ds(pl.multiple_of(end, SUBLANES), size), :],
                                                 zsem))
                    size //= 2
                return carry

            lax.fori_loop(0, n_e, per_expert, 0)

            def per_block(b, carry):
                go(pltpu.make_async_copy(zero_ref, xs_ref.at[pl.ds(pl.multiple_of(b * rows, rows), rows), :],
                                         zsem))
                return carry

            lax.fori_loop(nu_ref[0], n_blocks, per_block, 0)

        zero_fill(False)
        zero_fill(True)

    for k in range(top_k):
        pltpu.make_async_copy(h_ref, xs_ref.at[pl.ds(0, tm), :], sem).wait()


def _dispatch(counts, pstart, n_used, dest3, hp, n_slots):
    n, w = hp.shape
    n_tiles, top_k, tm = dest3.shape
    n_e = counts.shape[0]
    return pl.pallas_call(
        functools.partial(_dispatch_kernel, top_k=top_k, n_e=n_e, rows=MOE_ROWS),
        out_shape=jax.ShapeDtypeStruct((n_slots, w), U32),
        grid_spec=pltpu.PrefetchScalarGridSpec(
            num_scalar_prefetch=3,
            grid=(n_tiles,),
            in_specs=[pl.BlockSpec((1, top_k, tm), lambda i, c, p, u: (i, 0, 0),
                                   memory_space=pltpu.SMEM),
                      pl.BlockSpec((tm, w), lambda i, c, p, u: (i, 0))],
            out_specs=pl.BlockSpec(memory_space=pl.ANY),
            scratch_shapes=[pltpu.VMEM((MOE_ROWS, w), U32), pltpu.SemaphoreType.DMA(()),
                            pltpu.SemaphoreType.DMA(())],
        ),
        compiler_params=_params(("arbitrary",)),
        name="moe_dispatch",
    )(counts, pstart, n_used, dest3, hp)


def _swiglu_packed(xp, wg, wu, wd):
    xh, xl = _unpack_halves(xp)
    half = xh.shape[1]
    gate = (jnp.dot(xh, wg[:half], preferred_element_type=F32)
            + jnp.dot(xl, wg[half:], preferred_element_type=F32))
    up = (jnp.dot(xh, wu[:half], preferred_element_type=F32)
          + jnp.dot(xl, wu[half:], preferred_element_type=F32))
    hm = (gate * jax.nn.sigmoid(gate) * up).astype(BF16)
    return jnp.dot(hm, wd, preferred_element_type=F32)


def _expert_kernel(be_ref, nu_ref, x_ref, wg_ref, wu_ref, wd_ref, o_ref):
    b = pl.program_id(0)

    @pl.when(b < nu_ref[0])
    def _():
        y = _swiglu_packed(x_ref[...], wg_ref[0], wu_ref[0], wd_ref[0])
        o_ref[...] = _pack_halves(y)

    @pl.when(b >= nu_ref[0])
    def _():
        o_ref[...] = jnp.zeros_like(o_ref)


def _experts(block_e, n_used, xs, wg, wu, wd):
    n_slots, w = xs.shape
    n_e, d, de = wg.shape
    rows = MOE_ROWS
    n_blocks = n_slots // rows
    return pl.pallas_call(
        _expert_kernel,
        out_shape=jax.ShapeDtypeStruct((n_slots, w), U32),
        grid_spec=pltpu.PrefetchScalarGridSpec(
            num_scalar_prefetch=2,
            grid=(n_blocks,),
            in_specs=[pl.BlockSpec((rows, w), lambda b, be, nu: (jnp.minimum(b, nu[0] - 1), 0)),
                      pl.BlockSpec((1, d, de), lambda b, be, nu: (be[b], 0, 0)),
                      pl.BlockSpec((1, d, de), lambda b, be, nu: (be[b], 0, 0)),
                      pl.BlockSpec((1, de, d), lambda b, be, nu: (be[b], 0, 0))],
            out_specs=pl.BlockSpec((rows, w), lambda b, be, nu: (b, 0)),
        ),
        compiler_params=_params(("arbitrary",)),
        name="moe_experts",
    )(block_e, n_used, xs, wg, wu, wd)


def _shared_kernel(x_ref, wg_ref, wu_ref, wd_ref, o_ref):
    o_ref[...] = _swiglu_packed(x_ref[...], wg_ref[...], wu_ref[...], wd_ref[...])


def _shared_expert(hp, wg, wu, wd):
    n, w = hp.shape
    d, ds = wg.shape
    tm = _tile(n, 512)
    return pl.pallas_call(
        _shared_kernel,
        out_shape=jax.ShapeDtypeStruct((n, d), F32),
        grid=(n // tm,),
        in_specs=[pl.BlockSpec((tm, w), lambda i: (i, 0)),
                  pl.BlockSpec((d, ds), lambda i: (0, 0)),
                  pl.BlockSpec((d, ds), lambda i: (0, 0)),
                  pl.BlockSpec((ds, d), lambda i: (0, 0))],
        out_specs=pl.BlockSpec((tm, d), lambda i: (i, 0)),
        compiler_params=_params(("arbitrary",)),
        name="moe_shared",
    )(hp, wg, wu, wd)


def _combine_kernel(dest_ref, next_ref, w_ref, x_ref, sh_ref, mod_ref, g_ref, ys_ref, o_ref, buf_ref, sems,
                    *, top_k):
    tm, d = x_ref.shape
    half = d // 2
    i = pl.program_id(0)
    slot = i % 2

    def gather(idx_ref, slot):
        def issue(r, carry):
            for k in range(top_k):
                pltpu.make_async_copy(ys_ref.at[pl.ds(idx_ref[0, k, r], 1), :],
                                      buf_ref.at[slot, k, pl.ds(r, 1), :],
                                      sems.at[slot]).start(priority=k % 2)
            return carry
        lax.fori_loop(0, tm, issue, 0)

    @pl.when(i == 0)
    def _():
        gather(dest_ref, 0)

    @pl.when(i + 1 < pl.num_programs(0))
    def _():
        gather(next_ref, 1 - slot)

    for k in range(top_k):
        pltpu.make_async_copy(ys_ref.at[pl.ds(0, tm), :], buf_ref.at[slot, k], sems.at[slot]).wait()

    acc_h = jnp.zeros((tm, half), F32)
    acc_l = jnp.zeros((tm, half), F32)
    for k in range(top_k):
        p = buf_ref[slot, k]
        wk = w_ref[:, k:k + 1]
        acc_h = acc_h + wk * pltpu.bitcast(p & jnp.uint32(HI_MASK), F32)
        acc_l = acc_l + wk * pltpu.bitcast(p << 16, F32)
    gate = mod_ref[0, 5:6, :]
    x_h = x_ref[:, :half] + gate[:, :half] * (acc_h + sh_ref[:, :half])
    x_l = x_ref[:, half:] + gate[:, half:] * (acc_l + sh_ref[:, half:])
    ms = (jnp.sum(x_h * x_h, axis=1, keepdims=True) + jnp.sum(x_l * x_l, axis=1, keepdims=True)) / d
    inv = lax.rsqrt(ms + NORM_EPS)
    o_ref[:, :half] = x_h * inv * g_ref[:, :half]
    o_ref[:, half:] = x_l * inv * g_ref[:, half:]


def _combine(dest3, wts_t, x2, shared, mod3, g_final, ys, seq):
    n, d = x2.shape
    n_tiles, top_k, tm = dest3.shape
    per_b = seq // tm
    return pl.pallas_call(
        functools.partial(_combine_kernel, top_k=top_k),
        out_shape=jax.ShapeDtypeStruct((n, d), F32),
        grid=(n_tiles,),
        in_specs=[pl.BlockSpec((1, top_k, tm), lambda i: (i, 0, 0), memory_space=pltpu.SMEM),
                  pl.BlockSpec((1, top_k, tm), lambda i: (jnp.minimum(i + 1, n_tiles - 1), 0, 0),
                               memory_space=pltpu.SMEM),
                  pl.BlockSpec((tm, top_k), lambda i: (i, 0)),
                  pl.BlockSpec((tm, d), lambda i: (i, 0)),
                  pl.BlockSpec((tm, d), lambda i: (i, 0)),
                  pl.BlockSpec((1, N_MOD, d), lambda i: (i // per_b, 0, 0)),
                  pl.BlockSpec((1, d), lambda i: (0, 0)),
                  pl.BlockSpec(memory_space=pl.ANY)],
        out_specs=pl.BlockSpec((tm, d), lambda i: (i, 0)),
        scratch_shapes=[pltpu.VMEM((2, top_k, tm, d // 2), U32), pltpu.SemaphoreType.DMA((2,))],
        compiler_params=_params(("arbitrary",)),
        name="moe_combine",
    )(dest3, dest3, wts_t, x2, shared, mod3, g_final.reshape(1, d), ys)


def _retile(a, tm):
    k, n = a.shape
    return a.reshape(k, n // tm, tm).transpose(1, 0, 2)


def kernel(x, c, w_ada, b_ada, g_mix, w_in, b_forget, lam_re, lam_im, log_dt, b_re, b_im, c_re, c_im, d_skip, w_glu, w_proj_attn, w_proj_ssm, w_out, g_ffn, w_router, router_bias, w_gate_e, w_up_e, w_down_e, w_gate_s, w_up_s, w_down_s, g_final):
    bsz, seq, d = x.shape
    n = bsz * seq
    depth = w_ada.shape[0]
    n_heads = b_forget.shape[1]
    attn_w = w_proj_attn.shape[1]
    dh = attn_w // n_heads
    ssm_w = w_proj_ssm.shape[1]
    n_e = w_router.shape[2]

    assert depth == 1, "the final rms_norm is fused into the last layer's combine kernel"
    c_pad = jnp.zeros((8, d), F32).at[:bsz].set(c)
    x2 = x.reshape(n, d)
    out = None
    for l in range(depth):
        mod = _ada(c_pad, w_ada[l], b_ada[l])
        mod3 = mod[:bsz].reshape(bsz, N_MOD, d)

        h = _norm_modulate(x2, g_mix[l], mod3, seq, 0, 1)
        w = w_in[l]
        o_f = 3 * attn_w
        o_u = o_f + n_heads
        o_ga = o_u + ssm_w
        c_u = o_f
        c_ga = c_u + ssm_w
        c_f = c_ga + 2 * d
        w_all = jnp.concatenate([w[:, :o_f], w[:, o_u:], w[:, o_f:o_u],
                                 jnp.zeros((d, LANES - n_heads), F32)], axis=1).astype(BF16)
        q_scale = jnp.concatenate([jnp.full((attn_w,), dh ** -0.5 * LOG2E, F32),
                                   jnp.ones((2 * attn_w,), F32)]).reshape(1, o_f)
        qkv = _mm([h], [(0, w_all, 0)], [_row_extra(q_scale)],
                  lambda accs, ex: accs[0] * ex[0][...], o_f, BF16, "proj_qkv")
        f_logit = _mm([h], [(0, w_all, c_f)], [], lambda accs, ex: accs[0], LANES, F32, "proj_forget")
        u = _mm([h], [(0, w_all, c_u)], [], lambda accs, ex: accs[0], ssm_w, F32, "proj_u")
        gates = _mm([h], [(0, w_all, c_ga)], [],
                    lambda accs, ex: jax.nn.sigmoid(accs[0]), 2 * d, BF16, "proj_gates")

        b_row = jnp.zeros((1, LANES), F32).at[0, :n_heads].set(b_forget[l])
        cum = _forget_cumsum(f_logit.reshape(bsz, seq, LANES), b_row)[:, :, :n_heads]
        attn, (wg_e, wu_e, wd_e) = _attention(qkv.reshape(bsz, seq, o_f), cum, n_heads, dh,
                                              side=(w_gate_e[l], w_up_e[l], w_down_e[l]))
        attn = attn.reshape(n, attn_w)

        sg = _s5_ssm(u, seq, lam_re[l], lam_im[l], log_dt[l], b_re[l], b_im[l],
                     c_re[l], c_im[l], d_skip[l])
        w_g = w_glu[l].astype(BF16)
        ssm = _mm([sg], [(0, w_g, 0), (0, w_g, ssm_w)], [],
                  lambda accs, ex: accs[0] * jax.nn.sigmoid(accs[1]), ssm_w, BF16, "ssm_glu")

        merged = _mm([attn, ssm],
                     [(0, w_proj_attn[l].astype(BF16), 0), (1, w_proj_ssm[l].astype(BF16), 0)],
                     [_tile_extra(gates), _tile_extra(gates, d)],
                     lambda accs, ex: (ex[0][...].astype(F32) * accs[0]
                                       + ex[1][...].astype(F32) * accs[1]),
                     d, BF16, "merge_proj")
        x2 = _mm([merged], [(0, w_out[l].astype(BF16), 0)], [_tile_extra(x2), _mod_extra(mod3, seq)],
                 lambda accs, ex: ex[0][...] + ex[1][0, 2:3, :] * accs[0], d, F32, "out_proj",
                 row_unit=seq)

        hp, idx_t, wts_t, rank_t, cnt = _router(x2, g_ffn[l], mod3, seq, w_router[l], router_bias[l])
        counts = cnt[:, 0].astype(jnp.int32)
        padded = (counts + MOE_ROWS - 1) // MOE_ROWS * MOE_ROWS
        pend = jnp.cumsum(padded)
        pstart = pend - padded
        e_ids = jnp.arange(n_e, dtype=jnp.int32)
        first = jnp.sum(jnp.where(idx_t[None] == e_ids[:, None, None], pstart[:, None, None], 0), axis=0)
        dest = first + rank_t
        n_blocks = -(-(n * TOP_K + n_e * (MOE_ROWS - 1)) // MOE_ROWS)
        n_used = (pend[-1] // MOE_ROWS).astype(jnp.int32).reshape(1)
        blk = jnp.minimum(jnp.arange(n_blocks, dtype=jnp.int32), n_used - 1)
        block_e = jnp.minimum(jnp.sum((blk * MOE_ROWS)[:, None] >= pend[None, :], axis=1),
                              n_e - 1).astype(jnp.int32)
        tm_d = _tile(seq, 256)
        xs = _dispatch(counts, pstart.astype(jnp.int32), n_used, _retile(dest, tm_d), hp,
                       n_blocks * MOE_ROWS)
        ys = _experts(block_e, n_used, xs, wg_e, wu_e, wd_e)
        shared = _shared_expert(hp, w_gate_s[l].astype(BF16), w_up_s[l].astype(BF16),
                                w_down_s[l].astype(BF16))
        tm_c = _tile(seq, 128)
        out = _combine(_retile(dest, tm_c), wts_t.T, x2, shared, mod3, g_final, ys, seq)
    return out.reshape(bsz, seq, d)
```

```python
import functools
import math

import jax
import jax.numpy as jnp
from jax import lax
from jax.experimental import pallas as pl
from jax.experimental.pallas import tpu as pltpu

TOP_K = 8
ROUTED_SCALE = 2.5
NORM_EPS = 1e-6
N_MOD = 6
SSM_SUB = 8
SSM_ROWS = 256
MOE_ROWS = 256
ATTN_SUB_ROWS = 512
LOG2E = 1.4426950408889634

LANES = 128
SUBLANES = 8
VMEM_LIMIT = 56 * 1024 * 1024

F32 = jnp.float32
BF16 = jnp.bfloat16
U32 = jnp.uint32
HI_MASK = 0xFFFF0000


def _tile(n, pref):
    if n <= pref:
        return n
    t = pref
    while n % t:
        t //= 2
    return t


def _params(sem):
    return pltpu.CompilerParams(dimension_semantics=sem, vmem_limit_bytes=VMEM_LIMIT)


def _pack_halves(y):
    w = y.shape[1] // 2
    bits = pltpu.bitcast(y.astype(BF16).astype(F32), U32)
    return bits[:, :w] | (bits[:, w:] >> 16)


def _unpack_halves(p):
    hi = pltpu.bitcast(p & jnp.uint32(HI_MASK), F32)
    lo = pltpu.bitcast(p << 16, F32)
    return hi.astype(BF16), lo.astype(BF16)


def _ada_kernel(c_ref, w_ref, b_ref, o_ref):
    c = c_ref[...]
    cond = c * jax.nn.sigmoid(c)
    o_ref[...] = jnp.dot(cond.astype(BF16), w_ref[...].astype(BF16),
                         preferred_element_type=F32) + b_ref[...]


def _ada(c_pad, w_ada, b_ada):
    rows, d = c_pad.shape
    n = w_ada.shape[1]
    tn = _tile(n, 512)
    return pl.pallas_call(
        _ada_kernel,
        out_shape=jax.ShapeDtypeStruct((rows, n), F32),
        grid=(n // tn,),
        in_specs=[pl.BlockSpec((rows, d), lambda j: (0, 0)),
                  pl.BlockSpec((d, tn), lambda j: (0, j)),
                  pl.BlockSpec((1, tn), lambda j: (0, j))],
        out_specs=pl.BlockSpec((rows, tn), lambda j: (0, j)),
        compiler_params=_params(("arbitrary",)),
        name="ada_mod",
    )(c_pad, w_ada, b_ada.reshape(1, n))


def _norm_mod(x, g_row, mod_ref, shift_row, scale_row):
    ms = jnp.mean(x * x, axis=-1, keepdims=True)
    y = x * lax.rsqrt(ms + NORM_EPS) * g_row
    sc = mod_ref[0, scale_row:scale_row + 1, :]
    sh = mod_ref[0, shift_row:shift_row + 1, :]
    return y * (1.0 + sc) + sh


def _norm_mod_kernel(x_ref, g_ref, mod_ref, o_ref, *, shift_row, scale_row):
    o_ref[...] = _norm_mod(x_ref[...], g_ref[...], mod_ref, shift_row, scale_row).astype(o_ref.dtype)


def _norm_modulate(x2, g, mod3, seq, shift_row, scale_row):
    n, d = x2.shape
    tm = _tile(seq, 256)
    per_b = seq // tm
    return pl.pallas_call(
        functools.partial(_norm_mod_kernel, shift_row=shift_row, scale_row=scale_row),
        out_shape=jax.ShapeDtypeStruct((n, d), BF16),
        grid=(n // tm,),
        in_specs=[pl.BlockSpec((tm, d), lambda i: (i, 0)),
                  pl.BlockSpec((1, d), lambda i: (0, 0)),
                  pl.BlockSpec((1, N_MOD, d), lambda i: (i // per_b, 0, 0))],
        out_specs=pl.BlockSpec((tm, d), lambda i: (i, 0)),
        compiler_params=_params(("arbitrary",)),
        name="norm_mod",
    )(x2, g.reshape(1, d), mod3)


def _mm_kernel(*refs, n_a, a_of_b, epilogue):
    o_ref = refs[-1]
    n_b = len(a_of_b)
    lhs = [refs[a][...].astype(BF16) for a in range(n_a)]
    accs = [jnp.dot(lhs[a_of_b[p]], refs[n_a + p][...], preferred_element_type=F32)
            for p in range(n_b)]
    extras = refs[n_a + n_b:-1]
    o_ref[...] = epilogue(accs, extras).astype(o_ref.dtype)


def _mm(a_list, b_list, extras, epilogue, n_out, out_dtype, name, row_unit=None,
        tm_pref=1024, tn_pref=1024):
    m = a_list[0].shape[0]
    tm = _tile(row_unit or m, tm_pref)
    tn = _tile(n_out, tn_pref)
    in_specs, args = [], []
    for a in a_list:
        in_specs.append(pl.BlockSpec((tm, a.shape[1]), lambda i, j: (i, 0)))
        args.append(a)
    for _, b, col in b_list:
        assert col % tn == 0, "weight sections must start on a column-tile boundary"
        in_specs.append(pl.BlockSpec((b.shape[0], tn), lambda i, j, off=col // tn: (0, j + off)))
        args.append(b)
    for arr, bshape, imap in extras:
        in_specs.append(pl.BlockSpec(bshape(tm, tn), imap(tm, tn)))
        args.append(arr)
    return pl.pallas_call(
        functools.partial(_mm_kernel, n_a=len(a_list), a_of_b=tuple(ai for ai, _, _ in b_list),
                          epilogue=epilogue),
        out_shape=jax.ShapeDtypeStruct((m, n_out), out_dtype),
        grid=(m // tm, n_out // tn),
        in_specs=in_specs,
        out_specs=pl.BlockSpec((tm, tn), lambda i, j: (i, j)),
        compiler_params=_params(("arbitrary", "arbitrary")),
        name=name,
    )(*args)


def _row_extra(row):
    return (row, lambda tm, tn: (1, tn), lambda tm, tn: (lambda i, j: (0, j)))


def _tile_extra(arr, col=0):
    return (arr, lambda tm, tn: (tm, tn), lambda tm, tn: (lambda i, j: (i, j + col // tn)))


def _mod_extra(mod3, seq):
    return (mod3, lambda tm, tn: (1, N_MOD, tn),
            lambda tm, tn: (lambda i, j: (i // (seq // tm), 0, j)))


def _cum_kernel(f_ref, b_ref, o_ref, *, chunk):
    seq = f_ref.shape[1]
    x = f_ref[0] + b_ref[...]
    lf = jnp.minimum(x, 0.0) - jnp.log(1.0 + jnp.exp(-jnp.abs(x)))
    o_ref[0] = lf
    r = lax.broadcasted_iota(jnp.int32, (chunk, chunk), 0)
    c = lax.broadcasted_iota(jnp.int32, (chunk, chunk), 1)
    tri = jnp.where(c <= r, 1.0, 0.0).astype(F32)

    def body(i, carry):
        start = pl.multiple_of(i * chunk, chunk)
        blk = o_ref[0, pl.ds(start, chunk), :]
        cs = jnp.dot(tri, blk, precision=lax.Precision.HIGHEST,
                     preferred_element_type=F32) + carry
        o_ref[0, pl.ds(start, chunk), :] = cs
        return cs[chunk - 1:chunk, :]

    lax.fori_loop(0, seq // chunk, body, jnp.zeros((1, f_ref.shape[2]), F32))


def _forget_cumsum(f3, b_row):
    bsz, seq, w = f3.shape
    chunk = _tile(seq, 128)
    return pl.pallas_call(
        functools.partial(_cum_kernel, chunk=chunk),
        out_shape=jax.ShapeDtypeStruct((bsz, seq, w), F32),
        grid=(bsz,),
        in_specs=[pl.BlockSpec((1, seq, w), lambda b: (b, 0, 0)),
                  pl.BlockSpec((1, w), lambda b: (0, 0))],
        out_specs=pl.BlockSpec((1, seq, w), lambda b: (b, 0, 0)),
        compiler_params=_params(("arbitrary",)),
        name="forget_cumsum",
    )(f3, b_row)


def _attn_kernel(c0_ref, *refs, blk, n_sub, n_heads, n_q, n_side, paired):
    n_qb = 2 if paired else 1
    q_refs = refs[:n_qb]
    k_ref, v_ref, ck_ref = refs[n_qb:n_qb + 3]
    side_in = refs[n_qb + 3:n_qb + 3 + n_side]
    o_refs = refs[n_qb + 3 + n_side:2 * n_qb + 3 + n_side]
    side_out = refs[2 * n_qb + 3 + n_side:2 * n_qb + 3 + 2 * n_side]
    sa_ref, sb_ref = refs[2 * n_qb + 3 + 2 * n_side:]
    b = pl.program_id(0)
    h = pl.program_id(1)
    step = pl.program_id(2)
    for w_ref, wo_ref in zip(side_in, side_out):
        wo_ref[...] = w_ref[...].astype(wo_ref.dtype)
    blocks = [step, n_q - 1 - step] if paired else [step]
    for i, q_ref, o_ref in zip(blocks, q_refs, o_refs):
        _attn_sweep(i, c0_ref[(b * n_heads + h) * n_q + i], q_ref, k_ref, v_ref, ck_ref, o_ref,
                    sa_ref, sb_ref, blk=blk, n_sub=n_sub)


def _attn_sweep(i, c0, q_ref, k_ref, v_ref, ck_ref, o_ref, sa_ref, sb_ref, *, blk, n_sub):
    sub = blk // n_sub
    qs = [q_ref[0, a * sub:(a + 1) * sub, :] for a in range(n_sub)]

    def logits(j, s_ref):
        kb = k_ref[0, pl.ds(pl.multiple_of(j * blk, blk), blk), :]
        bias = (c0 - ck_ref[0, 0, j]) * LOG2E
        for a, q in enumerate(qs):
            s_ref[a * sub:(a + 1) * sub, :] = lax.dot_general(
                q, kb, (((1,), (1,)), ((), ())), preferred_element_type=F32) + bias

    def update(j, s_ref, state, diagonal=False):
        vb = v_ref[0, pl.ds(pl.multiple_of(j * blk, blk), blk), :]
        dh = vb.shape[1]
        vb = jnp.concatenate([vb, jnp.ones_like(vb)], axis=1)
        out = []
        for a, (m, l, acc) in enumerate(state):
            t = s_ref[a * sub:(a + 1) * sub, :]
            if diagonal:
                row = lax.broadcasted_iota(jnp.int32, (sub, blk), 0)
                col = lax.broadcasted_iota(jnp.int32, (sub, blk), 1)
                t = jnp.where(col <= row + a * sub, t, -jnp.inf)
            m_new = jnp.maximum(m, jnp.max(t, axis=1, keepdims=True))
            alpha = jnp.exp2(m - m_new)
            p = jnp.exp2(t - m_new)
            pv = jnp.dot(p.astype(BF16), vb, preferred_element_type=F32)
            l = alpha * l + pv[:, dh:dh + 1]
            acc = alpha * acc + pv[:, :dh]
            out.append((m_new, l, acc))
        return tuple(out)

    def body(pair, state):
        j = 2 * pair
        logits(j + 1, sb_ref)
        state = update(j, sa_ref, state)
        logits(j + 2, sa_ref)
        return update(j + 1, sb_ref, state)

    def body2(quad, state):
        return body(2 * quad + 1, body(2 * quad, state))

    init = tuple((jnp.full((sub, 1), -jnp.inf, F32), jnp.zeros((sub, 1), F32),
                  jnp.zeros((sub, q_ref.shape[2]), F32)) for _ in range(n_sub))
    logits(0, sa_ref)
    state = lax.fori_loop(0, i // 4, body2, init)
    state = lax.fori_loop(2 * (i // 4), i // 2, body, state)

    def finish(state):
        for a, (m, l, acc) in enumerate(state):
            o_ref[0, a * sub:(a + 1) * sub, :] = (acc / l).astype(o_ref.dtype)

    @pl.when(i % 2 == 0)
    def _():
        finish(update(i, sa_ref, state, diagonal=True))

    @pl.when(i % 2 == 1)
    def _():
        logits(i, sb_ref)
        finish(update(i, sb_ref, update(i - 1, sa_ref, state), diagonal=True))


def _attention(qkv3, cum, n_heads, dh, side=()):
    bsz, seq, _ = qkv3.shape
    blk = _tile(seq, 512)
    n_q = seq // blk
    paired = n_q % 2 == 0
    n_qs = n_q // 2 if paired else n_q
    n_steps = bsz * n_heads * n_qs
    cum_t = cum.transpose(0, 2, 1)
    ck = cum_t.reshape(bsz, n_heads, n_q, 1, blk)
    c0 = cum_t[:, :, ::blk].reshape(-1)
    side2 = [w.reshape(-1, w.shape[-1]) for w in side]
    bf16_rows = 2 * SUBLANES
    assert all(w.shape[0] % (n_steps * bf16_rows) == 0 for w in side2)

    def slab(w):
        return pl.BlockSpec((w.shape[0] // n_steps, w.shape[1]),
                            lambda b, h, i, c: ((b * n_heads + h) * n_qs + i, 0))

    q_specs = [pl.BlockSpec((1, blk, dh), lambda b, h, i, c: (b, i, h))]
    o_specs = [pl.BlockSpec((1, blk, dh), lambda b, h, i, c: (b, i, h))]
    o_rows = [seq]
    if paired:
        q_specs.append(pl.BlockSpec((1, blk, dh), lambda b, h, i, c: (b, n_q - 1 - i, h)))
        o_specs.append(pl.BlockSpec((1, blk, dh), lambda b, h, i, c: (b, n_qs - 1 - i, h)))
        o_rows = [seq // 2, seq // 2]
    n_o = len(o_specs)
    out = pl.pallas_call(
        functools.partial(_attn_kernel, blk=blk, n_sub=max(1, blk // ATTN_SUB_ROWS),
                          n_heads=n_heads, n_q=n_q, n_side=len(side2), paired=paired),
        out_shape=[jax.ShapeDtypeStruct((bsz, r, n_heads * dh), BF16) for r in o_rows]
        + [jax.ShapeDtypeStruct(w.shape, BF16) for w in side2],
        grid_spec=pltpu.PrefetchScalarGridSpec(
            num_scalar_prefetch=1,
            grid=(bsz, n_heads, n_qs),
            in_specs=q_specs
            + [pl.BlockSpec((1, seq, dh), lambda b, h, i, c: (b, 0, n_heads + h)),
               pl.BlockSpec((1, seq, dh), lambda b, h, i, c: (b, 0, 2 * n_heads + h)),
               pl.BlockSpec((1, 1, n_q, 1, blk), lambda b, h, i, c: (b, h, 0, 0, 0))]
            + [slab(w) for w in side2],
            out_specs=o_specs + [slab(w) for w in side2],
            scratch_shapes=[pltpu.VMEM((blk, blk), F32), pltpu.VMEM((blk, blk), F32)],
        ),
        compiler_params=_params(("arbitrary", "arbitrary", "arbitrary")),
        name="fox_attention",
    )(c0, *([qkv3] * n_o), qkv3, qkv3, ck, *side2)
    attn = out[0] if n_o == 1 else jnp.concatenate(out[:n_o], axis=1)
    return attn, [o.reshape(w.shape) for o, w in zip(out[n_o:], side)]


def _gelu_tanh(x):
    return 0.5 * x * (1.0 + jnp.tanh(math.sqrt(2.0 / math.pi) * (x + 0.044715 * (x * x * x))))


def _ssm_kernel(u_ref, wm_ref, wz_ref, wy_ref, apow_ref, dd_ref, o_ref, xcat_ref, carry_ref,
                *, sub, n_levels):
    rb = u_ref.shape[0] // sub

    @pl.when(pl.program_id(2) == 0)
    def _():
        carry_ref[...] = jnp.zeros_like(carry_ref)

    for i in range(sub):
        xcat_ref[:, i * LANES:(i + 1) * LANES] = u_ref[pl.ds(i, rb, stride=sub), :].astype(BF16)
    x = xcat_ref[...]
    y = jnp.dot(x, wm_ref[0], preferred_element_type=F32)
    s = jnp.dot(x, wz_ref[0], preferred_element_type=F32)
    half = s.shape[1] // 2
    row = lax.broadcasted_iota(jnp.int32, s.shape, 0)
    carry = carry_ref[...]

    a1 = apow_ref[0, 0:1, :]
    a2 = apow_ref[0, 1:2, :]

    def times_a(v):
        return a1 * v + a2 * pltpu.roll(v, half, axis=1)

    s = s + jnp.where(row == 0, times_a(carry), 0.0)
    for lvl in range(n_levels):
        shift = 1 << lvl
        s = s + times_a(jnp.where(row >= shift, pltpu.roll(s, shift, axis=0), 0.0))
        a1, a2 = a1 * a1 - a2 * a2, 2.0 * a1 * a2
    s_in = jnp.where(row >= 1, pltpu.roll(s, 1, axis=0), carry)
    carry_ref[...] = s[rb - 1:rb, :]
    y = y + jnp.dot(s_in.astype(BF16), wy_ref[0], preferred_element_type=F32)
    for j in range(sub):
        yj = y[:, j * LANES:(j + 1) * LANES] + dd_ref[0] * u_ref[pl.ds(j, rb, stride=sub), :]
        o_ref[pl.ds(j, rb, stride=sub), :] = _gelu_tanh(yj)


def _ssm_tables(lam_re, lam_im, log_dt, b_re, b_im, c_re, c_im, d_skip):
    g, p = lam_re.shape
    hg = b_re.shape[-1]
    sub = SSM_SUB
    dt = jnp.exp(log_dt)[:, None]
    mag = jnp.exp(lam_re * dt)
    a_re = mag * jnp.cos(lam_im * dt)
    a_im = mag * jnp.sin(lam_im * dt)
    den = lam_re * lam_re + lam_im * lam_im
    k_re = ((a_re - 1) * lam_re + a_im * lam_im) / den
    k_im = (a_im * lam_re - (a_re - 1) * lam_im) / den
    bb_re = k_re[..., None] * b_re - k_im[..., None] * b_im
    bb_im = k_re[..., None] * b_im + k_im[..., None] * b_re

    tau = jnp.arange(sub + 1, dtype=F32)[:, None, None]
    pmag = jnp.exp(tau * (lam_re * dt))
    pw_re = pmag * jnp.cos(tau * (lam_im * dt))
    pw_im = pmag * jnp.sin(tau * (lam_im * dt))

    ca_re = c_re[None] * pw_re[:sub, :, None, :] - c_im[None] * pw_im[:sub, :, None, :]
    ca_im = c_re[None] * pw_im[:sub, :, None, :] + c_im[None] * pw_re[:sub, :, None, :]
    kk = (jnp.einsum('tgop,gpi->tgio', ca_re, bb_re, precision=lax.Precision.HIGHEST)
          - jnp.einsum('tgop,gpi->tgio', ca_im, bb_im, precision=lax.Precision.HIGHEST))

    gt = LANES // hg
    nt = g // gt
    wide = sub * LANES
    sw = gt * 2 * p

    def block_diag(x, rows_per_group, cols_per_group):
        n_r, n_c = gt * rows_per_group, gt * cols_per_group
        r = lax.broadcasted_iota(jnp.int32, (n_r, n_c), 0) // rows_per_group
        c = lax.broadcasted_iota(jnp.int32, (n_r, n_c), 1) // cols_per_group
        return jnp.where(r == c, jnp.tile(x, (1,) * (x.ndim - 1) + (gt,)), 0.0)

    bd = block_diag(kk.reshape(sub, nt, LANES, hg), hg, hg).astype(BF16)
    zero = jnp.zeros_like(bd[0])
    wm = jnp.stack([jnp.concatenate([zero] * i + [bd[k] for k in range(sub - i)], axis=-1)
                    for i in range(sub)], axis=1).reshape(nt, wide, wide)

    rev_re = pw_re[:sub][::-1]
    rev_im = pw_im[:sub][::-1]
    z_re = rev_re[..., None] * bb_re[None] - rev_im[..., None] * bb_im[None]
    z_im = rev_re[..., None] * bb_im[None] + rev_im[..., None] * bb_re[None]
    def z_part(z):
        z = z.transpose(0, 1, 3, 2).reshape(sub, nt, LANES, p).transpose(1, 0, 2, 3)
        return block_diag(z, hg, p).astype(BF16)

    wz = jnp.concatenate([z_part(z_re), z_part(z_im)], axis=-1).reshape(nt, wide, sw)

    cb_re = c_re[None] * pw_re[1:, :, None, :] - c_im[None] * pw_im[1:, :, None, :]
    cb_im = c_re[None] * pw_im[1:, :, None, :] + c_im[None] * pw_re[1:, :, None, :]
    def y_part(cb):
        cb = cb.transpose(0, 1, 3, 2).reshape(sub, nt, gt * p, hg)
        bd_j = block_diag(cb, p, hg).astype(BF16)
        return jnp.concatenate([bd_j[j] for j in range(sub)], axis=-1)

    wy = jnp.concatenate([y_part(cb_re), y_part(-cb_im)], axis=1)

    lr, li = pw_re[sub].reshape(nt, gt * p), pw_im[sub].reshape(nt, gt * p)
    apow = jnp.stack([jnp.concatenate([lr, lr], axis=1), jnp.concatenate([-li, li], axis=1)], axis=1)
    apow = jnp.pad(apow, ((0, 0), (0, SUBLANES - 2), (0, 0)))
    dd = d_skip.reshape(nt, 1, LANES)
    return wm, wz, wy, apow, dd


def _s5_ssm(u, seq, lam_re, lam_im, log_dt, b_re, b_im, c_re, c_im, d_skip):
    n, width = u.shape
    g, p = lam_re.shape
    hg = width // g
    sub = SSM_SUB
    assert LANES % hg == 0 and width % LANES == 0
    gt = LANES // hg
    nt = width // LANES
    rb = _tile(seq // sub, SSM_ROWS)
    n_rb = seq // (rb * sub)
    n_levels = max(1, (rb - 1).bit_length())
    wm, wz, wy, apow, dd = _ssm_tables(lam_re, lam_im, log_dt, b_re, b_im, c_re, c_im, d_skip)
    wide = sub * LANES
    sw = gt * 2 * p
    return pl.pallas_call(
        functools.partial(_ssm_kernel, sub=sub, n_levels=n_levels),
        out_shape=jax.ShapeDtypeStruct((n, width), F32),
        grid=(nt, n // seq, n_rb),
        in_specs=[pl.BlockSpec((rb * sub, LANES), lambda t, b, k: (b * n_rb + k, t)),
                  pl.BlockSpec((1, wide, wide), lambda t, b, k: (t, 0, 0)),
                  pl.BlockSpec((1, wide, sw), lambda t, b, k: (t, 0, 0)),
                  pl.BlockSpec((1, sw, wide), lambda t, b, k: (t, 0, 0)),
                  pl.BlockSpec((1, apow.shape[1], sw), lambda t, b, k: (t, 0, 0)),
                  pl.BlockSpec((1, 1, LANES), lambda t, b, k: (t, 0, 0))],
        out_specs=pl.BlockSpec((rb * sub, LANES), lambda t, b, k: (b * n_rb + k, t)),
        scratch_shapes=[pltpu.VMEM((rb, wide), BF16), pltpu.VMEM((1, sw), F32)],
        compiler_params=_params(("arbitrary", "arbitrary", "arbitrary")),
        name="s5_ssm",
    )(u, wm, wz, wy, apow, dd)


def _router_kernel(x_ref, g_ref, mod_ref, wr_ref, rb_ref,
                   hp_ref, idx_ref, wts_ref, rank_ref, cnt_ref, carry_ref, *, top_k):
    i = pl.program_id(0)

    @pl.when(i == 0)
    def _():
        carry_ref[...] = jnp.zeros_like(carry_ref)

    h = _norm_mod(x_ref[...], g_ref[...], mod_ref, 3, 4)
    hp_ref[...] = _pack_halves(h)
    h_hi = h.astype(BF16)
    h_lo = (h - h_hi.astype(F32)).astype(BF16)
    nt = (((1,), (1,)), ((), ()))
    logits = (lax.dot_general(wr_ref[0], h_hi, nt, preferred_element_type=F32)
              + lax.dot_general(wr_ref[0], h_lo, nt, preferred_element_type=F32)
              + lax.dot_general(wr_ref[1], h_hi, nt, preferred_element_type=F32))
    scores = jax.nn.sigmoid(logits)
    sel = scores + rb_ref[...]
    n_e, tm = sel.shape
    e_iota = lax.broadcasted_iota(jnp.int32, (n_e, tm), 0)
    hits, vals = [], []
    for _ in range(top_k):
        mx = jnp.max(sel, axis=0, keepdims=True)
        ik = jnp.min(jnp.where(sel == mx, e_iota, n_e), axis=0, keepdims=True)
        hit = e_iota == ik
        vals.append(jnp.sum(jnp.where(hit, scores, 0.0), axis=0, keepdims=True))
        sel = jnp.where(hit, -jnp.inf, sel)
        hits.append(hit)
        idx_ref[pl.ds(len(hits) - 1, 1), :] = ik
    tot = vals[0]
    for v in vals[1:]:
        tot = tot + v
    onehot = jnp.zeros((n_e, tm), F32)
    for k in range(top_k):
        wts_ref[pl.ds(k, 1), :] = vals[k] / tot * ROUTED_SCALE
        onehot = onehot + jnp.where(hits[k], 1.0, 0.0)
    r = lax.broadcasted_iota(jnp.int32, (tm, tm), 0)
    c = lax.broadcasted_iota(jnp.int32, (tm, tm), 1)
    upper = jnp.where(r < c, 1.0, 0.0).astype(BF16)
    base = jnp.dot(onehot.astype(BF16), upper, preferred_element_type=F32) + carry_ref[...]
    for k in range(top_k):
        rk = jnp.sum(jnp.where(hits[k], base, 0.0), axis=0, keepdims=True)
        rank_ref[pl.ds(k, 1), :] = rk.astype(jnp.int32)
    carry_ref[...] = carry_ref[...] + jnp.sum(onehot, axis=1, keepdims=True)
    cnt_ref[...] = carry_ref[...]


def _router(x2, g, mod3, seq, w_router, router_bias):
    n, d = x2.shape
    n_e = w_router.shape[1]
    tm = _tile(seq, 256)
    per_b = seq // tm
    wr_hi = w_router.T.astype(BF16)
    wr_lo = (w_router.T - wr_hi.astype(F32)).astype(BF16)
    wr_t = jnp.stack([wr_hi, wr_lo])
    return pl.pallas_call(
        functools.partial(_router_kernel, top_k=TOP_K),
        out_shape=[jax.ShapeDtypeStruct((n, d // 2), U32),
                   jax.ShapeDtypeStruct((TOP_K, n), jnp.int32),
                   jax.ShapeDtypeStruct((TOP_K, n), F32),
                   jax.ShapeDtypeStruct((TOP_K, n), jnp.int32),
                   jax.ShapeDtypeStruct((n_e, 1), F32)],
        grid=(n // tm,),
        in_specs=[pl.BlockSpec((tm, d), lambda i: (i, 0)),
                  pl.BlockSpec((1, d), lambda i: (0, 0)),
                  pl.BlockSpec((1, N_MOD, d), lambda i: (i // per_b, 0, 0)),
                  pl.BlockSpec((2, n_e, d), lambda i: (0, 0, 0)),
                  pl.BlockSpec((n_e, 1), lambda i: (0, 0))],
        out_specs=[pl.BlockSpec((tm, d // 2), lambda i: (i, 0)),
                   pl.BlockSpec((TOP_K, tm), lambda i: (0, i)),
                   pl.BlockSpec((TOP_K, tm), lambda i: (0, i)),
                   pl.BlockSpec((TOP_K, tm), lambda i: (0, i)),
                   pl.BlockSpec((n_e, 1), lambda i: (0, 0))],
        scratch_shapes=[pltpu.VMEM((n_e, 1), F32)],
        compiler_params=_params(("arbitrary",)),
        name="moe_router",
    )(x2, g.reshape(1, d), mod3, wr_t, router_bias.reshape(n_e, 1))


def _dispatch_kernel(cnt_ref, pstart_ref, nu_ref, dest_ref, h_ref, xs_ref, zero_ref, sem, zsem,
                     *, top_k, n_e, rows):
    i = pl.program_id(0)
    tm = h_ref.shape[0]
    n_blocks = xs_ref.shape[0] // rows

    def issue(r, carry):
        for k in range(top_k):
            pltpu.make_async_copy(h_ref.at[pl.ds(r, 1), :],
                                  xs_ref.at[pl.ds(dest_ref[0, k, r], 1), :], sem).start(priority=k % 2)
        return carry

    lax.fori_loop(0, tm, issue, 0)

    @pl.when(i == 0)
    def _():
        zero_ref[...] = jnp.zeros_like(zero_ref)

        def zero_fill(wait):
            def go(copy):
                if wait:
                    copy.wait()
                else:
                    copy.start()

            def per_expert(e, carry):
                cnt = cnt_ref[e]
                pad = (rows - cnt % rows) % rows
                first = pstart_ref[e] + cnt

                def one_row(r, c):
                    go(pltpu.make_async_copy(zero_ref.at[pl.ds(0, 1), :],
                                             xs_ref.at[pl.ds(first + r, 1), :], zsem))
                    return c

                lax.fori_loop(0, pad % SUBLANES, one_row, 0)
                end = first + pad
                size = rows // 2
                while size >= SUBLANES:
                    end = end - (pad & size)

                    @pl.when((pad & size) != 0)
                    def _(size=size, end=end):
                        go(pltpu.make_async_copy(zero_ref.at[pl.ds(0, size), :],
                                                 xs_ref.at[pl.ds(pl.multiple_of(end, SUBLANES), size), :],
                                                 zsem))
                    size //= 2
                return carry

            lax.fori_loop(0, n_e, per_expert, 0)

            def per_block(b, carry):
                go(pltpu.make_async_copy(zero_ref, xs_ref.at[pl.ds(pl.multiple_of(b * rows, rows), rows), :],
                                         zsem))
                return carry

            lax.fori_loop(nu_ref[0], n_blocks, per_block, 0)

        zero_fill(False)
        zero_fill(True)

    for k in range(top_k):
        pltpu.make_async_copy(h_ref, xs_ref.at[pl.ds(0, tm), :], sem).wait()


def _dispatch(counts, pstart, n_used, dest3, hp, n_slots):
    n, w = hp.shape
    n_tiles, top_k, tm = dest3.shape
    n_e = counts.shape[0]
    return pl.pallas_call(
        functools.partial(_dispatch_kernel, top_k=top_k, n_e=n_e, rows=MOE_ROWS),
        out_shape=jax.ShapeDtypeStruct((n_slots, w), U32),
        grid_spec=pltpu.PrefetchScalarGridSpec(
            num_scalar_prefetch=3,
            grid=(n_tiles,),
            in_specs=[pl.BlockSpec((1, top_k, tm), lambda i, c, p, u: (i, 0, 0),
                                   memory_space=pltpu.SMEM),
                      pl.BlockSpec((tm, w), lambda i, c, p, u: (i, 0))],
            out_specs=pl.BlockSpec(memory_space=pl.ANY),
            scratch_shapes=[pltpu.VMEM((MOE_ROWS, w), U32), pltpu.SemaphoreType.DMA(()),
                            pltpu.SemaphoreType.DMA(())],
        ),
        compiler_params=_params(("arbitrary",)),
        name="moe_dispatch",
    )(counts, pstart, n_used, dest3, hp)


def _swiglu_packed(xp, wg, wu, wd):
    xh, xl = _unpack_halves(xp)
    half = xh.shape[1]
    gate = (jnp.dot(xh, wg[:half], preferred_element_type=F32)
            + jnp.dot(xl, wg[half:], preferred_element_type=F32))
    up = (jnp.dot(xh, wu[:half], preferred_element_type=F32)
          + jnp.dot(xl, wu[half:], preferred_element_type=F32))
    hm = (gate * jax.nn.sigmoid(gate) * up).astype(BF16)
    return jnp.dot(hm, wd, preferred_element_type=F32)


def _expert_kernel(be_ref, nu_ref, x_ref, wg_ref, wu_ref, wd_ref, o_ref):
    b = pl.program_id(0)

    @pl.when(b < nu_ref[0])
    def _():
        y = _swiglu_packed(x_ref[...], wg_ref[0], wu_ref[0], wd_ref[0])
        o_ref[...] = _pack_halves(y)

    @pl.when(b >= nu_ref[0])
    def _():
        o_ref[...] = jnp.zeros_like(o_ref)


def _experts(block_e, n_used, xs, wg, wu, wd):
    n_slots, w = xs.shape
    n_e, d, de = wg.shape
    rows = MOE_ROWS
    n_blocks = n_slots // rows
    return pl.pallas_call(
        _expert_kernel,
        out_shape=jax.ShapeDtypeStruct((n_slots, w), U32),
        grid_spec=pltpu.PrefetchScalarGridSpec(
            num_scalar_prefetch=2,
            grid=(n_blocks,),
            in_specs=[pl.BlockSpec((rows, w), lambda b, be, nu: (jnp.minimum(b, nu[0] - 1), 0)),
                      pl.BlockSpec((1, d, de), lambda b, be, nu: (be[b], 0, 0)),
                      pl.BlockSpec((1, d, de), lambda b, be, nu: (be[b], 0, 0)),
                      pl.BlockSpec((1, de, d), lambda b, be, nu: (be[b], 0, 0))],
            out_specs=pl.BlockSpec((rows, w), lambda b, be, nu: (b, 0)),
        ),
        compiler_params=_params(("arbitrary",)),
        name="moe_experts",
    )(block_e, n_used, xs, wg, wu, wd)


def _shared_kernel(x_ref, wg_ref, wu_ref, wd_ref, o_ref):
    o_ref[...] = _swiglu_packed(x_ref[...], wg_ref[...], wu_ref[...], wd_ref[...])


def _shared_expert(hp, wg, wu, wd):
    n, w = hp.shape
    d, ds = wg.shape
    tm = _tile(n, 512)
    return pl.pallas_call(
        _shared_kernel,
        out_shape=jax.ShapeDtypeStruct((n, d), F32),
        grid=(n // tm,),
        in_specs=[pl.BlockSpec((tm, w), lambda i: (i, 0)),
                  pl.BlockSpec((d, ds), lambda i: (0, 0)),
                  pl.BlockSpec((d, ds), lambda i: (0, 0)),
                  pl.BlockSpec((ds, d), lambda i: (0, 0))],
        out_specs=pl.BlockSpec((tm, d), lambda i: (i, 0)),
        compiler_params=_params(("arbitrary",)),
        name="moe_shared",
    )(hp, wg, wu, wd)


def _combine_kernel(dest_ref, next_ref, w_ref, x_ref, sh_ref, mod_ref, g_ref, ys_ref, o_ref, buf_ref, sems,
                    *, top_k):
    tm, d = x_ref.shape
    half = d // 2
    i = pl.program_id(0)
    slot = i % 2

    def gather(idx_ref, slot):
        def issue(r, carry):
            for k in range(top_k):
                pltpu.make_async_copy(ys_ref.at[pl.ds(idx_ref[0, k, r], 1), :],
                                      buf_ref.at[slot, k, pl.ds(r, 1), :],
                                      sems.at[slot]).start(priority=k % 2)
            return carry
        lax.fori_loop(0, tm, issue, 0)

    @pl.when(i == 0)
    def _():
        gather(dest_ref, 0)

    @pl.when(i + 1 < pl.num_programs(0))
    def _():
        gather(next_ref, 1 - slot)

    for k in range(top_k):
        pltpu.make_async_copy(ys_ref.at[pl.ds(0, tm), :], buf_ref.at[slot, k], sems.at[slot]).wait()

    acc_h = jnp.zeros((tm, half), F32)
    acc_l = jnp.zeros((tm, half), F32)
    for k in range(top_k):
        p = buf_ref[slot, k]
        wk = w_ref[:, k:k + 1]
        acc_h = acc_h + wk * pltpu.bitcast(p & jnp.uint32(HI_MASK), F32)
        acc_l = acc_l + wk * pltpu.bitcast(p << 16, F32)
    gate = mod_ref[0, 5:6, :]
    x_h = x_ref[:, :half] + gate[:, :half] * (acc_h + sh_ref[:, :half])
    x_l = x_ref[:, half:] + gate[:, half:] * (acc_l + sh_ref[:, half:])
    ms = (jnp.sum(x_h * x_h, axis=1, keepdims=True) + jnp.sum(x_l * x_l, axis=1, keepdims=True)) / d
    inv = lax.rsqrt(ms + NORM_EPS)
    o_ref[:, :half] = x_h * inv * g_ref[:, :half]
    o_ref[:, half:] = x_l * inv * g_ref[:, half:]


def _combine(dest3, wts_t, x2, shared, mod3, g_final, ys, seq):
    n, d = x2.shape
    n_tiles, top_k, tm = dest3.shape
    per_b = seq // tm
    return pl.pallas_call(
        functools.partial(_combine_kernel, top_k=top_k),
        out_shape=jax.ShapeDtypeStruct((n, d), F32),
        grid=(n_tiles,),
        in_specs=[pl.BlockSpec((1, top_k, tm), lambda i: (i, 0, 0), memory_space=pltpu.SMEM),
                  pl.BlockSpec((1, top_k, tm), lambda i: (jnp.minimum(i + 1, n_tiles - 1), 0, 0),
                               memory_space=pltpu.SMEM),
                  pl.BlockSpec((tm, top_k), lambda i: (i, 0)),
                  pl.BlockSpec((tm, d), lambda i: (i, 0)),
                  pl.BlockSpec((tm, d), lambda i: (i, 0)),
                  pl.BlockSpec((1, N_MOD, d), lambda i: (i // per_b, 0, 0)),
                  pl.BlockSpec((1, d), lambda i: (0, 0)),
                  pl.BlockSpec(memory_space=pl.ANY)],
        out_specs=pl.BlockSpec((tm, d), lambda i: (i, 0)),
        scratch_shapes=[pltpu.VMEM((2, top_k, tm, d // 2), U32), pltpu.SemaphoreType.DMA((2,))],
        compiler_params=_params(("arbitrary",)),
        name="moe_combine",
    )(dest3, dest3, wts_t, x2, shared, mod3, g_final.reshape(1, d), ys)


def _retile(a, tm):
    k, n = a.shape
    return a.reshape(k, n // tm, tm).transpose(1, 0, 2)


def kernel(x, c, w_ada, b_ada, g_mix, w_in, b_forget, lam_re, lam_im, log_dt, b_re, b_im, c_re, c_im, d_skip, w_glu, w_proj_attn, w_proj_ssm, w_out, g_ffn, w_router, router_bias, w_gate_e, w_up_e, w_down_e, w_gate_s, w_up_s, w_down_s, g_final):
    bsz, seq, d = x.shape
    n = bsz * seq
    depth = w_ada.shape[0]
    n_heads = b_forget.shape[1]
    attn_w = w_proj_attn.shape[1]
    dh = attn_w // n_heads
    ssm_w = w_proj_ssm.shape[1]
    n_e = w_router.shape[2]

    assert depth == 1, "the final rms_norm is fused into the last layer's combine kernel"
    c_pad = jnp.zeros((8, d), F32).at[:bsz].set(c)
    x2 = x.reshape(n, d)
    out = None
    for l in range(depth):
        mod = _ada(c_pad, w_ada[l], b_ada[l])
        mod3 = mod[:bsz].reshape(bsz, N_MOD, d)

        h = _norm_modulate(x2, g_mix[l], mod3, seq, 0, 1)
        w = w_in[l]
        o_f = 3 * attn_w
        o_u = o_f + n_heads
        o_ga = o_u + ssm_w
        w_qkv = w[:, :o_f].astype(BF16)
        q_scale = jnp.concatenate([jnp.full((attn_w,), dh ** -0.5 * LOG2E, F32),
                                   jnp.ones((2 * attn_w,), F32)]).reshape(1, o_f)
        qkv = _mm([h], [(0, w_qkv, 0)], [_row_extra(q_scale)],
                  lambda accs, ex: accs[0] * ex[0][...], o_f, BF16, "proj_qkv")
        w_f = jnp.zeros((d, LANES), F32).at[:, :n_heads].set(w[:, o_f:o_u]).astype(BF16)
        f_logit = _mm([h], [(0, w_f, 0)], [], lambda accs, ex: accs[0], LANES, F32, "proj_forget")
        u = _mm([h], [(0, w[:, o_u:o_ga].astype(BF16), 0)], [], lambda accs, ex: accs[0],
                ssm_w, F32, "proj_u")
        gates = _mm([h], [(0, w[:, o_ga:].astype(BF16), 0)], [],
                    lambda accs, ex: jax.nn.sigmoid(accs[0]), 2 * d, BF16, "proj_gates")

        b_row = jnp.zeros((1, LANES), F32).at[0, :n_heads].set(b_forget[l])
        cum = _forget_cumsum(f_logit.reshape(bsz, seq, LANES), b_row)[:, :, :n_heads]
        attn, (wg_e, wu_e, wd_e) = _attention(qkv.reshape(bsz, seq, o_f), cum, n_heads, dh,
                                              side=(w_gate_e[l], w_up_e[l], w_down_e[l]))
        attn = attn.reshape(n, attn_w)

        sg = _s5_ssm(u, seq, lam_re[l], lam_im[l], log_dt[l], b_re[l], b_im[l],
                     c_re[l], c_im[l], d_skip[l])
        w_g = w_glu[l].astype(BF16)
        ssm = _mm([sg], [(0, w_g, 0), (0, w_g, ssm_w)], [],
                  lambda accs, ex: accs[0] * jax.nn.sigmoid(accs[1]), ssm_w, BF16, "ssm_glu")

        merged = _mm([attn, ssm],
                     [(0, w_proj_attn[l].astype(BF16), 0), (1, w_proj_ssm[l].astype(BF16), 0)],
                     [_tile_extra(gates), _tile_extra(gates, d)],
                     lambda accs, ex: (ex[0][...].astype(F32) * accs[0]
                                       + ex[1][...].astype(F32) * accs[1]),
                     d, BF16, "merge_proj")
        x2 = _mm([merged], [(0, w_out[l].astype(BF16), 0)], [_tile_extra(x2), _mod_extra(mod3, seq)],
                 lambda accs, ex: ex[0][...] + ex[1][0, 2:3, :] * accs[0], d, F32, "out_proj",
                 row_unit=seq)

        hp, idx_t, wts_t, rank_t, cnt = _router(x2, g_ffn[l], mod3, seq, w_router[l], router_bias[l])
        counts = cnt[:, 0].astype(jnp.int32)
        padded = (counts + MOE_ROWS - 1) // MOE_ROWS * MOE_ROWS
        pend = jnp.cumsum(padded)
        pstart = pend - padded
        e_ids = jnp.arange(n_e, dtype=jnp.int32)
        first = jnp.sum(jnp.where(idx_t[None] == e_ids[:, None, None], pstart[:, None, None], 0), axis=0)
        dest = first + rank_t
        n_blocks = -(-(n * TOP_K + n_e * (MOE_ROWS - 1)) // MOE_ROWS)
        n_used = (pend[-1] // MOE_ROWS).astype(jnp.int32).reshape(1)
        blk = jnp.minimum(jnp.arange(n_blocks, dtype=jnp.int32), n_used - 1)
        block_e = jnp.minimum(jnp.sum((blk * MOE_ROWS)[:, None] >= pend[None, :], axis=1),
                              n_e - 1).astype(jnp.int32)
        tm_d = _tile(seq, 256)
        xs = _dispatch(counts, pstart.astype(jnp.int32), n_used, _retile(dest, tm_d), hp,
                       n_blocks * MOE_ROWS)
        ys = _experts(block_e, n_used, xs, wg_e, wu_e, wd_e)
        shared = _shared_expert(hp, w_gate_s[l].astype(BF16), w_up_s[l].astype(BF16),
                                w_down_s[l].astype(BF16))
        tm_c = _tile(seq, 128)
        out = _combine(_retile(dest, tm_c), wts_t.T, x2, shared, mod3, g_final, ys, seq)
    return out.reshape(bsz, seq, d)
```
